```python
import jax, jax.numpy as jnp
from jax import lax
import numpy as np

D_MODEL = 1024
BATCH = 2
SEQ = 8192
DEPTH = 1
DEC_BATCH = 1
DEC_SEQ = 16384
PAST_LEN = 128

GRID_W = 64
NA_HEADS = 16
NA_HEAD_DIM = 64
NA_WIDTH = NA_HEADS * NA_HEAD_DIM
NA_KH = 8
NA_KW = 16
RET_HEADS = 4
RET_KEY_DIM = D_MODEL // RET_HEADS
RET_VAL_DIM = 2 * RET_KEY_DIM
RET_QK_WIDTH = RET_HEADS * RET_KEY_DIM
RET_V_WIDTH = RET_HEADS * RET_VAL_DIM
RET_CHUNK = 128
ROPE_BASE = 10000.0
FFN_HIDDEN = ((8 * D_MODEL + 3 * 256 - 1) // (3 * 256)) * 256
IN_WIDTH = 3 * NA_WIDTH + 2 * RET_QK_WIDTH + 2 * RET_V_WIDTH + 2 * D_MODEL
N_MOD = 6
EPS = 1e-6

kernel_name = "hybrid_natten_retention_encoder"


def rmsnorm(x, g):
    xf = x.astype(jnp.float32)
    y = xf * lax.rsqrt(jnp.mean(xf * xf, axis=-1, keepdims=True) + EPS) * g.astype(jnp.float32)
    return y.astype(x.dtype)


def neighbourhood_attention(q, k, v, rpb):
    b, n, h, dh = q.shape
    rows = n // GRID_W
    kh = min(NA_KH, rows)
    kw = NA_KW
    q = q.reshape(b, rows, GRID_W, h, dh) * (dh ** -0.5)
    k = k.reshape(b, rows, GRID_W, h, dh)
    v = v.reshape(b, rows, GRID_W, h, dh)
    cols = np.arange(GRID_W)
    col_start = np.clip(cols - kw // 2, 0, GRID_W - kw)
    col_idx = col_start[:, None] + np.arange(kw)[None, :]
    col_rel = col_idx - cols[:, None] + (NA_KW - 1)
    col_bias = rpb[:, :, col_rel].astype(jnp.float32)

    def row_block(r):
        rs = jnp.clip(r - kh // 2, 0, rows - kh)
        kr = lax.dynamic_slice_in_dim(k, rs, kh, axis=1)
        vr = lax.dynamic_slice_in_dim(v, rs, kh, axis=1)
        kg = kr[:, :, col_idx]
        vg = vr[:, :, col_idx]
        qr = lax.dynamic_index_in_dim(q, r, axis=1, keepdims=False)
        s = jnp.einsum('bchd,bicjhd->bhcij', qr, kg, preferred_element_type=jnp.float32)
        row_rel = rs + jnp.arange(kh) - r + (NA_KH - 1)
        bias = jnp.take(col_bias, row_rel, axis=1)
        s = s + bias.transpose(0, 2, 1, 3)[None]
        p = jax.nn.softmax(s.reshape(b, h, GRID_W, kh * kw), axis=-1).reshape(b, h, GRID_W, kh, kw)
        return jnp.einsum('bhcij,bicjhd->bchd', p.astype(v.dtype), vg)

    out = lax.map(row_block, jnp.arange(rows))
    return out.transpose(1, 0, 2, 3, 4).reshape(b, n, h * dh)


def rotary(x, pos):
    d = x.shape[-1]
    half = d // 2
    inv = 1.0 / (ROPE_BASE ** (jnp.arange(half, dtype=jnp.float32) / half))
    ang = pos[:, None] * inv[None, :]
    cos, sin = jnp.cos(ang), jnp.sin(ang)
    x1, x2 = x[..., :half], x[..., half:]
    return jnp.concatenate([x1 * cos - x2 * sin, x1 * sin + x2 * cos], axis=-1)


def retention_direction(q, k, v, log_gamma, strict):
    b, h, n, dk = q.shape
    dv = v.shape[-1]
    c = RET_CHUNK
    nc = n // c
    pos = jnp.arange(c, dtype=jnp.float32)
    rel = pos[:, None] - pos[None, :]
    mask = rel > 0 if strict else rel >= 0
    lg = log_gamma[:, None, None]
    dmat = jnp.where(mask[None], jnp.exp(jnp.where(mask, rel, 0.0)[None] * lg), 0.0)
    xi = jnp.exp((pos + 1.0)[None, :] * log_gamma[:, None])[None, :, :, None]
    zeta = jnp.exp((c - 1.0 - pos)[None, :] * log_gamma[:, None])[None, :, :, None]
    chunk_decay = jnp.exp(c * log_gamma)[None, :, None, None]

    def to_chunks(t):
        return t.reshape(b, h, nc, c, t.shape[-1]).transpose(2, 0, 1, 3, 4)

    def step(state, xs):
        qc, kc, vc = xs
        scores = jnp.einsum('bhqd,bhkd->bhqk', qc, kc) * dmat[None]
        intra = jnp.einsum('bhqk,bhkv->bhqv', scores, vc)
        cross = jnp.einsum('bhqd,bhdv->bhqv', qc, state) * xi
        new_state = state * chunk_decay + jnp.einsum('bhkd,bhkv->bhdv', kc * zeta, vc)
        return new_state, intra + cross

    state0 = jnp.zeros((b, h, dk, dv), jnp.float32)
    _, out = lax.scan(step, state0, (to_chunks(q), to_chunks(k), to_chunks(v)))
    return out.transpose(1, 2, 0, 3, 4).reshape(b, h, n, dv)


def bidirectional_retention(q, k, v, decay_logit, gn_gain):
    b, n, h, dk = q.shape
    pos = jnp.arange(n, dtype=jnp.float32)
    qf = rotary(q.astype(jnp.float32).transpose(0, 2, 1, 3), pos)
    kf = rotary(k.astype(jnp.float32).transpose(0, 2, 1, 3), pos) * (dk ** -0.5)
    vf = v.astype(jnp.float32).transpose(0, 2, 1, 3)
    lg = jax.nn.log_sigmoid(decay_logit.astype(jnp.float32))
    fwd = retention_direction(qf, kf, vf, lg[0], False)
    bwd = jnp.flip(retention_direction(jnp.flip(qf, 2), jnp.flip(kf, 2), jnp.flip(vf, 2), lg[1], True), 2)
    y = fwd + bwd
    mu = jnp.mean(y, axis=-1, keepdims=True)
    var = jnp.mean(jnp.square(y - mu), axis=-1, keepdims=True)
    y = (y - mu) * lax.rsqrt(var + EPS)
    y = y.transpose(0, 2, 1, 3).reshape(b, n, h * y.shape[-1]) * gn_gain.astype(jnp.float32)
    return y.astype(q.dtype)


def encoder_layer(x, c, w_mod, b_mod, g_pre_mix, w_in, rpb, ret_decay_logit, ret_gn,
                  w_up_na, w_up_ret, w_out, g_post_mix, g_pre_ffn, w_ffn_in, w_ffn_out, g_post_ffn):
    b, n, d = x.shape
    mod = jnp.dot(jax.nn.silu(c), w_mod) + b_mod
    sh1, sc1, gt1, sh2, sc2, gt2 = [m[:, None, :] for m in jnp.split(mod, N_MOD, axis=-1)]

    h = rmsnorm(x, g_pre_mix) * (1.0 + sc1) + sh1
    proj = jnp.dot(h, w_in)
    cuts = list(np.cumsum([NA_WIDTH, NA_WIDTH, NA_WIDTH, RET_QK_WIDTH, RET_QK_WIDTH,
                           RET_V_WIDTH, RET_V_WIDTH, D_MODEL])[:])
    na_q, na_k, na_v, r_q, r_k, r_v, r_g, g_na, g_ret = jnp.split(proj, cuts, axis=-1)
    y_na = neighbourhood_attention(na_q.reshape(b, n, NA_HEADS, NA_HEAD_DIM),
                                   na_k.reshape(b, n, NA_HEADS, NA_HEAD_DIM),
                                   na_v.reshape(b, n, NA_HEADS, NA_HEAD_DIM), rpb)
    y_ret = bidirectional_retention(r_q.reshape(b, n, RET_HEADS, RET_KEY_DIM),
                                    r_k.reshape(b, n, RET_HEADS, RET_KEY_DIM),
                                    r_v.reshape(b, n, RET_HEADS, RET_VAL_DIM),
                                    ret_decay_logit, ret_gn)
    y_ret = jax.nn.silu(r_g) * y_ret
    merged = jax.nn.sigmoid(g_na) * jnp.dot(y_na, w_up_na) + jax.nn.sigmoid(g_ret) * jnp.dot(y_ret, w_up_ret)
    out = jnp.dot(merged, w_out)
    x = x + gt1 * rmsnorm(out, g_post_mix)

    h = rmsnorm(x, g_pre_ffn) * (1.0 + sc2) + sh2
    a, gte = jnp.split(jnp.dot(h, w_ffn_in), 2, axis=-1)
    f = jnp.dot(jax.nn.silu(a) * gte, w_ffn_out)
    x = x + gt2 * rmsnorm(f, g_post_ffn)
    return x


def setup_inputs(seed: int = 0) -> dict:
    key = jax.random.key(seed)
    ks = jax.random.split(key, 24)
    f32 = jnp.float32

    def nrm(k, shape, scale):
        return jax.random.normal(k, shape, f32) * scale

    gammas = 1.0 - 2.0 ** (-5.0 - np.arange(RET_HEADS, dtype=np.float32))
    base_logit = jnp.asarray(np.log(gammas / (1.0 - gammas)), f32)
    decay_logit = base_logit[None, None, :] + nrm(ks[4], (DEPTH, 2, RET_HEADS), 0.1)
    return {
        "x_prompt": nrm(ks[0], (BATCH, SEQ, D_MODEL), 1.0),
        "x_sample": nrm(ks[1], (DEC_BATCH, DEC_SEQ, D_MODEL), 1.0),
        "c_prompt": nrm(ks[2], (BATCH, D_MODEL), 1.0),
        "c_sample": nrm(ks[3], (DEC_BATCH, D_MODEL), 1.0),
        "w_mod": nrm(ks[5], (DEPTH, D_MODEL, N_MOD * D_MODEL), 0.5 * D_MODEL ** -0.5),
        "b_mod": nrm(ks[6], (DEPTH, N_MOD * D_MODEL), 0.02),
        "g_pre_mix": 1.0 + nrm(ks[7], (DEPTH, D_MODEL), 0.05),
        "w_in": nrm(ks[8], (DEPTH, D_MODEL, IN_WIDTH), D_MODEL ** -0.5),
        "rpb": nrm(ks[9], (DEPTH, NA_HEADS, 2 * NA_KH - 1, 2 * NA_KW - 1), 0.5),
        "ret_decay_logit": decay_logit,
        "ret_gn": 1.0 + nrm(ks[10], (DEPTH, RET_V_WIDTH), 0.05),
        "w_up_na": nrm(ks[11], (DEPTH, NA_WIDTH, D_MODEL), NA_WIDTH ** -0.5),
        "w_up_ret": nrm(ks[12], (DEPTH, RET_V_WIDTH, D_MODEL), RET_V_WIDTH ** -0.5),
        "w_out": nrm(ks[13], (DEPTH, D_MODEL, D_MODEL), D_MODEL ** -0.5),
        "g_post_mix": 1.0 + nrm(ks[14], (DEPTH, D_MODEL), 0.05),
        "g_pre_ffn": 1.0 + nrm(ks[15], (DEPTH, D_MODEL), 0.05),
        "w_ffn_in": nrm(ks[16], (DEPTH, D_MODEL, 2 * FFN_HIDDEN), D_MODEL ** -0.5),
        "w_ffn_out": nrm(ks[17], (DEPTH, FFN_HIDDEN, D_MODEL), FFN_HIDDEN ** -0.5),
        "g_post_ffn": 1.0 + nrm(ks[18], (DEPTH, D_MODEL), 0.05),
    }


def reference(x_prompt, x_sample, c_prompt, c_sample, w_mod, b_mod, g_pre_mix, w_in, rpb,
              ret_decay_logit, ret_gn, w_up_na, w_up_ret, w_out, g_post_mix, g_pre_ffn,
              w_ffn_in, w_ffn_out, g_post_ffn):
    y_prompt = x_prompt
    y_sample = x_sample
    for l in range(DEPTH):
        layer_w = (w_mod[l], b_mod[l], g_pre_mix[l], w_in[l], rpb[l], ret_decay_logit[l], ret_gn[l],
                   w_up_na[l], w_up_ret[l], w_out[l], g_post_mix[l], g_pre_ffn[l], w_ffn_in[l],
                   w_ffn_out[l], g_post_ffn[l])
        y_prompt = encoder_layer(y_prompt, c_prompt, *layer_w)
        y_sample = encoder_layer(y_sample, c_sample, *layer_w)
    return (y_prompt, y_sample)
```

```python
import functools

import numpy as np
import jax
import jax.numpy as jnp
from jax import lax
from jax.experimental import pallas as pl
from jax.experimental.pallas import tpu as pltpu

D_MODEL = 1024
GRID_W = 64
NA_HEADS = 16
NA_HEAD_DIM = 64
NA_KH = 8
NA_KW = 16
RET_HEADS = 4
RET_KEY_DIM = 256
RET_VAL_DIM = 512
ROPE_BASE = 10000.0
FFN_HIDDEN = 2816
IN_WIDTH = 11264
N_MOD = 6
EPS = 1e-6

BF16 = jnp.bfloat16
F32 = jnp.float32

COL_NA_Q, COL_NA_K, COL_NA_V = 0, 1, 2
COL_RET_Q, COL_RET_K = 12, 16
COL_RET_V, COL_RET_G = 10, 14
COL_G_NA, COL_G_RET = 9, 10

NEG_BIAS = -1e30
NA_ROWS_PER_STEP = 8
NA_PAIRS = NA_HEADS // 2
RET_CHUNK = 256
RET_BLOCK = 2048
VMEM_LIMIT = 56 * 1024 * 1024


def _params(semantics, vmem=VMEM_LIMIT):
    return pltpu.CompilerParams(dimension_semantics=semantics, vmem_limit_bytes=vmem)


def _silu(x):
    return x * jax.nn.sigmoid(x)


def _mod_kernel(c_ref, w_ref, b_ref, o_ref):
    s = _silu(c_ref[...]).astype(BF16)
    o_ref[...] = jnp.dot(s, w_ref[...].astype(BF16), preferred_element_type=F32) + b_ref[...]


def _modulation(c, w_mod, b_mod):
    rows = c.shape[0]
    width = w_mod.shape[1]
    tn = 1536
    return pl.pallas_call(
        _mod_kernel,
        grid=(width // tn,),
        in_specs=[pl.BlockSpec((rows, D_MODEL), lambda j: (0, 0)),
                  pl.BlockSpec((D_MODEL, tn), lambda j: (0, j)),
                  pl.BlockSpec((1, tn), lambda j: (0, j))],
        out_specs=pl.BlockSpec((rows, tn), lambda j: (0, j)),
        out_shape=jax.ShapeDtypeStruct((rows, width), F32),
        compiler_params=_params(("arbitrary",)),
        name="modulation",
    )(c, w_mod, b_mod.reshape(1, width))


def _rope_kernel(inv_ref, cos_ref, sin_ref):
    tb = cos_ref.shape[0]
    pos = (pl.program_id(0) * tb + lax.broadcasted_iota(jnp.int32, cos_ref.shape, 0)).astype(F32)
    ang = pos * inv_ref[...]
    cos_ref[...] = jnp.cos(ang)
    sin_ref[...] = jnp.sin(ang)


def _rope_tables(n):
    half = RET_KEY_DIM // 2
    inv = 1.0 / (ROPE_BASE ** (jnp.arange(half, dtype=F32) / half))
    tb = 2048
    spec = pl.BlockSpec((tb, half), lambda i: (i, 0))
    return pl.pallas_call(
        _rope_kernel,
        grid=(n // tb,),
        in_specs=[pl.BlockSpec((1, half), lambda i: (0, 0))],
        out_specs=[spec, spec],
        out_shape=[jax.ShapeDtypeStruct((n, half), F32)] * 2,
        compiler_params=_params(("parallel",)),
        name="rope_tables",
    )(inv.reshape(1, half))


def _modulated_rmsnorm(x, gain, scale, shift):
    y = x * lax.rsqrt(jnp.mean(x * x, axis=-1, keepdims=True) + EPS) * gain
    return y * (1.0 + scale) + shift


def _inproj_kernel(x_ref, sc_ref, sh_ref, g_ref, w_ref, o_ref, h_ref):
    @pl.when(pl.program_id(1) == 0)
    def _():
        h = _modulated_rmsnorm(x_ref[...], g_ref[...], sc_ref[0], sh_ref[0])
        h_ref[...] = h.astype(BF16)

    o_ref[...] = jnp.dot(h_ref[...], w_ref[...], preferred_element_type=F32).astype(o_ref.dtype)


def _input_projection(x2d, n, scale, shift, gain, w_in):
    t = x2d.shape[0]
    tm, tn = 1024, 1024
    per_seq = n // tm
    vec = pl.BlockSpec((1, 1, D_MODEL), lambda i, j: (i // per_seq, 0, 0))
    return pl.pallas_call(
        _inproj_kernel,
        grid=(t // tm, IN_WIDTH // tn),
        in_specs=[pl.BlockSpec((tm, D_MODEL), lambda i, j: (i, 0)),
                  vec, vec,
                  pl.BlockSpec((1, D_MODEL), lambda i, j: (0, 0)),
                  pl.BlockSpec((D_MODEL, tn), lambda i, j: (0, j))],
        out_specs=pl.BlockSpec((tm, tn), lambda i, j: (i, j)),
        out_shape=jax.ShapeDtypeStruct((t, IN_WIDTH), BF16),
        scratch_shapes=[pltpu.VMEM((tm, D_MODEL), BF16)],
        compiler_params=_params(("parallel", "arbitrary")),
        name="input_projection",
    )(x2d, scale, shift, gain, w_in)


def _na_bias_table(rpb):
    w = GRID_W
    cols = np.arange(w)
    col_start = np.clip(cols - NA_KW // 2, 0, w - NA_KW)
    in_window = (cols[None, :] >= col_start[:, None]) & (cols[None, :] < col_start[:, None] + NA_KW)
    n_rows = 2 * NA_KH - 1
    pad_lo = (w - 1) - (NA_KW - 1)
    u = jnp.pad(rpb.astype(F32), ((0, 0), (0, 0), (pad_lo, 2 * w - (2 * NA_KW - 1) - pad_lo)))
    tiled = jnp.tile(u, (1, 1, w))
    toe = tiled[:, :, w - 1:w - 1 + w * (2 * w - 1)].reshape(NA_HEADS, n_rows, w, 2 * w - 1)[..., :w]
    masked = jnp.where(jnp.asarray(in_window)[None, None], toe, NEG_BIAS)
    pair = jnp.stack([masked[:, :n_rows - 1], masked[:, 1:]], axis=3)
    pair = pair.reshape(NA_PAIRS, 2, n_rows - 1, w, 2 * w)
    return pair.transpose(0, 2, 1, 3, 4).reshape(NA_PAIRS, n_rows - 1, 2 * w, 2 * w)


def _na_kernel(q_ref, k0_ref, k1_ref, k2_ref, v0_ref, v1_ref, v2_ref, tb_ref, o_ref, kw_ref, vw_ref,
               *, rows_total):
    i = pl.program_id(1)
    blk = NA_ROWS_PER_STEP * GRID_W
    for s, (kr, vr) in enumerate(((k0_ref, v0_ref), (k1_ref, v1_ref), (k2_ref, v2_ref))):
        kw_ref[s * blk:(s + 1) * blk, :] = kr[0]
        vw_ref[s * blk:(s + 1) * blk, :] = vr[0]

    w = GRID_W
    row = lax.broadcasted_iota(jnp.int32, (2 * w, 2 * w), 0)
    lane = lax.broadcasted_iota(jnp.int32, (2 * w, 2 * w), 1)
    own_head = (row < w) == (lane < w)
    first_head = lax.broadcasted_iota(jnp.int32, (w, 2 * w), 1) < w
    win = NA_KH * w

    def body(j, carry):
        r = i * NA_ROWS_PER_STEP + j
        rs = jnp.clip(r - NA_KH // 2, 0, rows_total - NA_KH)
        off = pl.multiple_of((rs - (i - 1) * NA_ROWS_PER_STEP) * w, w)
        rel0 = rs - r + (NA_KH - 1)
        qoff = pl.multiple_of(j * w, w)
        for p in range(NA_PAIRS):
            lanes = slice(p * 2 * w, (p + 1) * 2 * w)
            qp = q_ref[0, pl.ds(qoff, w), lanes].astype(F32) * (NA_HEAD_DIM ** -0.5)
            qs = jnp.where(own_head, jnp.concatenate([qp, qp], axis=0), 0.0).astype(BF16)
            kp = kw_ref[pl.ds(off, win), lanes]
            s = lax.dot_general(qs, kp, (((1,), (1,)), ((), ())), preferred_element_type=F32)
            bias = jnp.concatenate([tb_ref[p, rel0 + 2 * t] for t in range(NA_KH // 2)], axis=1)
            s = s + bias
            m = jnp.max(s, axis=1, keepdims=True)
            e = jnp.exp(s - m)
            l = jnp.sum(e, axis=1, keepdims=True)
            vp = vw_ref[pl.ds(off, win), lanes]
            o2 = jnp.dot(e.astype(BF16), vp, preferred_element_type=F32) / l
            o = jnp.where(first_head, o2[:w], o2[w:])
            o_ref[0, pl.ds(qoff, w), lanes] = o.astype(o_ref.dtype)
        return carry

    lax.fori_loop(0, NA_ROWS_PER_STEP, body, 0)


def _neighbourhood_attention(proj, bias_table):
    b, n, _ = proj.shape
    rows = n // GRID_W
    nblk = rows // NA_ROWS_PER_STEP
    blk = NA_ROWS_PER_STEP * GRID_W
    width = NA_HEADS * NA_HEAD_DIM

    def spec(col, shift):
        return pl.BlockSpec((1, blk, width),
                            lambda bi, i: (bi, jnp.clip(i + shift, 0, nblk - 1), col))

    return pl.pallas_call(
        functools.partial(_na_kernel, rows_total=rows),
        grid=(b, nblk),
        in_specs=[spec(COL_NA_Q, 0),
                  spec(COL_NA_K, -1), spec(COL_NA_K, 0), spec(COL_NA_K, 1),
                  spec(COL_NA_V, -1), spec(COL_NA_V, 0), spec(COL_NA_V, 1),
                  pl.BlockSpec(bias_table.shape, lambda bi, i: (0, 0, 0, 0))],
        out_specs=pl.BlockSpec((1, blk, width), lambda bi, i: (bi, i, 0)),
        out_shape=jax.ShapeDtypeStruct((b, n, width), BF16),
        scratch_shapes=[pltpu.VMEM((3 * blk, width), BF16), pltpu.VMEM((3 * blk, width), BF16)],
        compiler_params=_params(("parallel", "arbitrary")),
        name="neighbourhood_attention",
    )(proj, proj, proj, proj, proj, proj, proj, bias_table)


def _rotate(x, cos, sin):
    half = RET_KEY_DIM // 2
    x1, x2 = x[:, :half], x[:, half:]
    return jnp.concatenate([x1 * cos - x2 * sin, x1 * sin + x2 * cos], axis=1)


def _ret_kernel(*refs, backward):
    if backward:
        dl_ref, q_ref, k_ref, v_ref, cos_ref, sin_ref, fwd_ref, g_ref, gn_ref, o_ref, state_ref = refs
    else:
        dl_ref, q_ref, k_ref, v_ref, cos_ref, sin_ref, o_ref, state_ref = refs
    c = RET_CHUNK

    @pl.when(pl.program_id(2) == 0)
    def _():
        state_ref[...] = jnp.zeros_like(state_ref)

    logit = dl_ref[0, 0]
    lg = jnp.minimum(logit, 0.0) - jnp.log(1.0 + jnp.exp(-jnp.abs(logit)))
    row = lax.broadcasted_iota(jnp.int32, (c, c), 0)
    col = lax.broadcasted_iota(jnp.int32, (c, c), 1)
    pos = lax.broadcasted_iota(jnp.int32, (c, 1), 0).astype(F32)
    if backward:
        rel = (col - row).astype(F32)
        mask = rel > 0
        xi = jnp.exp((c - pos) * lg)
        zeta = jnp.exp(pos * lg)
    else:
        rel = (row - col).astype(F32)
        mask = rel >= 0
        xi = jnp.exp((pos + 1.0) * lg)
        zeta = jnp.exp((c - 1.0 - pos) * lg)
    decay = jnp.where(mask, jnp.exp(jnp.where(mask, rel, 0.0) * lg), 0.0)
    chunk_decay = jnp.exp(c * lg)

    nsub = q_ref.shape[1] // c
    order = range(nsub - 1, -1, -1) if backward else range(nsub)
    for u in order:
        sl = slice(u * c, (u + 1) * c)
        cos, sin = cos_ref[sl, :], sin_ref[sl, :]
        q = _rotate(q_ref[0, sl, :].astype(F32), cos, sin)
        k = _rotate(k_ref[0, sl, :].astype(F32), cos, sin) * (RET_KEY_DIM ** -0.5)
        v = v_ref[0, sl, :]
        qb = q.astype(BF16)
        scores = lax.dot_general(qb, k.astype(BF16), (((1,), (1,)), ((), ())),
                                 preferred_element_type=F32) * decay
        intra = jnp.dot(scores.astype(BF16), v, preferred_element_type=F32)
        state = state_ref[...]
        cross = jnp.dot(qb, state.astype(BF16), preferred_element_type=F32) * xi
        kz = (k * zeta).astype(BF16)
        update = lax.dot_general(kz, v, (((0,), (0,)), ((), ())), preferred_element_type=F32)
        state_ref[...] = state * chunk_decay + update
        y = intra + cross
        if backward:
            y = y + fwd_ref[0, sl, :].astype(F32)
            mu = jnp.mean(y, axis=-1, keepdims=True)
            yc = y - mu
            var = jnp.mean(yc * yc, axis=-1, keepdims=True)
            yn = yc * lax.rsqrt(var + EPS) * gn_ref[...]
            y = _silu(g_ref[0, sl, :].astype(F32)) * yn
        o_ref[0, sl, :] = y.astype(o_ref.dtype)


def _retention_pass(proj, decay_logit, cos, sin, fwd=None, gn=None):
    b, n, _ = proj.shape
    backward = fwd is not None
    nblk = n // RET_BLOCK

    def tok(ci):
        return nblk - 1 - ci if backward else ci

    in_specs = [pl.BlockSpec((1, 1, 1, 1), lambda bi, h, ci: (1 if backward else 0, h, 0, 0)),
                pl.BlockSpec((1, RET_BLOCK, RET_KEY_DIM), lambda bi, h, ci: (bi, tok(ci), COL_RET_Q + h)),
                pl.BlockSpec((1, RET_BLOCK, RET_KEY_DIM), lambda bi, h, ci: (bi, tok(ci), COL_RET_K + h)),
                pl.BlockSpec((1, RET_BLOCK, RET_VAL_DIM), lambda bi, h, ci: (bi, tok(ci), COL_RET_V + h)),
                pl.BlockSpec((RET_BLOCK, RET_KEY_DIM // 2), lambda bi, h, ci: (tok(ci), 0)),
                pl.BlockSpec((RET_BLOCK, RET_KEY_DIM // 2), lambda bi, h, ci: (tok(ci), 0))]
    args = [decay_logit.reshape(2, RET_HEADS, 1, 1), proj, proj, proj, cos, sin]
    out_spec = pl.BlockSpec((1, RET_BLOCK, RET_VAL_DIM), lambda bi, h, ci: (bi, tok(ci), h))
    if backward:
        in_specs += [out_spec,
                     pl.BlockSpec((1, RET_BLOCK, RET_VAL_DIM), lambda bi, h, ci: (bi, tok(ci), COL_RET_G + h)),
                     pl.BlockSpec((1, RET_VAL_DIM), lambda bi, h, ci: (0, h))]
        args += [fwd, proj, gn.reshape(1, RET_HEADS * RET_VAL_DIM)]
    return pl.pallas_call(
        functools.partial(_ret_kernel, backward=backward),
        grid=(b, RET_HEADS, nblk),
        in_specs=in_specs,
        out_specs=out_spec,
        out_shape=jax.ShapeDtypeStruct((b, n, RET_HEADS * RET_VAL_DIM), BF16),
        scratch_shapes=[pltpu.VMEM((RET_KEY_DIM, RET_VAL_DIM), F32)],
        compiler_params=_params(("parallel", "parallel", "arbitrary")),
        name="retention_bwd" if backward else "retention_fwd",
    )(*args)


def _merge_kernel(x_ref, yna_ref, yret_ref, gna_ref, gret_ref, wna_ref, wret_ref, wout_ref,
                  gpost_ref, gt_ref, o_ref):
    a = jnp.dot(yna_ref[...], wna_ref[...], preferred_element_type=F32)
    r = jnp.dot(yret_ref[...], wret_ref[...], preferred_element_type=F32)
    merged = jax.nn.sigmoid(gna_ref[...].astype(F32)) * a + jax.nn.sigmoid(gret_ref[...].astype(F32)) * r
    out = jnp.dot(merged.astype(BF16), wout_ref[...], preferred_element_type=F32)
    y = out * lax.rsqrt(jnp.mean(out * out, axis=-1, keepdims=True) + EPS) * gpost_ref[...]
    o_ref[...] = x_ref[...] + gt_ref[0] * y


def _merge(x2d, n, y_na, y_ret, proj2d, w_up_na, w_up_ret, w_out, g_post, gate):
    t = x2d.shape[0]
    tm = 512
    per_seq = n // tm
    ret_w = RET_HEADS * RET_VAL_DIM

    def const(shape):
        return pl.BlockSpec(shape, lambda i: (0, 0))

    return pl.pallas_call(
        _merge_kernel,
        grid=(t // tm,),
        in_specs=[pl.BlockSpec((tm, D_MODEL), lambda i: (i, 0)),
                  pl.BlockSpec((tm, D_MODEL), lambda i: (i, 0)),
                  pl.BlockSpec((tm, ret_w), lambda i: (i, 0)),
                  pl.BlockSpec((tm, D_MODEL), lambda i: (i, COL_G_NA)),
                  pl.BlockSpec((tm, D_MODEL), lambda i: (i, COL_G_RET)),
                  const((D_MODEL, D_MODEL)), const((ret_w, D_MODEL)), const((D_MODEL, D_MODEL)),
                  const((1, D_MODEL)),
                  pl.BlockSpec((1, 1, D_MODEL), lambda i: (i // per_seq, 0, 0))],
        out_specs=pl.BlockSpec((tm, D_MODEL), lambda i: (i, 0)),
        out_shape=jax.ShapeDtypeStruct((t, D_MODEL), F32),
        compiler_params=_params(("parallel",)),
        name="merge_out_projection",
    )(x2d, y_na, y_ret, proj2d, proj2d, w_up_na, w_up_ret, w_out, g_post, gate)


def _ffn_kernel(x_ref, sc_ref, sh_ref, gpre_ref, wa_ref, wg_ref, wout_ref, gpost_ref, gt_ref, o_ref,
                h_ref, acc_ref):
    j = pl.program_id(1)

    @pl.when(j == 0)
    def _():
        h = _modulated_rmsnorm(x_ref[...], gpre_ref[...], sc_ref[0], sh_ref[0])
        h_ref[...] = h.astype(BF16)

    h = h_ref[...]
    a = jnp.dot(h, wa_ref[...], preferred_element_type=F32)
    g = jnp.dot(h, wg_ref[...], preferred_element_type=F32)
    part = jnp.dot((_silu(a) * g).astype(BF16), wout_ref[...], preferred_element_type=F32)

    @pl.when(j == 0)
    def _():
        acc_ref[...] = part

    @pl.when(j > 0)
    def _():
        acc_ref[...] += part

    @pl.when(j == pl.num_programs(1) - 1)
    def _():
        f = acc_ref[...]
        y = f * lax.rsqrt(jnp.mean(f * f, axis=-1, keepdims=True) + EPS) * gpost_ref[...]
        o_ref[...] = x_ref[...] + gt_ref[0] * y


def _ffn(x2d, n, scale, shift, g_pre, w_ffn_in, w_ffn_out, g_post, gate):
    t = x2d.shape[0]
    tm = 512
    hc = 1408
    nhc = FFN_HIDDEN // hc
    per_seq = n // tm
    vec = pl.BlockSpec((1, 1, D_MODEL), lambda i, j: (i // per_seq, 0, 0))
    gain = pl.BlockSpec((1, D_MODEL), lambda i, j: (0, 0))
    return pl.pallas_call(
        _ffn_kernel,
        grid=(t // tm, nhc),
        in_specs=[pl.BlockSpec((tm, D_MODEL), lambda i, j: (i, 0)),
                  vec, vec, gain,
                  pl.BlockSpec((D_MODEL, hc), lambda i, j: (0, j)),
                  pl.BlockSpec((D_MODEL, hc), lambda i, j: (0, nhc + j)),
                  pl.BlockSpec((hc, D_MODEL), lambda i, j: (j, 0)),
                  gain, vec],
        out_specs=pl.BlockSpec((tm, D_MODEL), lambda i, j: (i, 0)),
        out_shape=jax.ShapeDtypeStruct((t, D_MODEL), F32),
        scratch_shapes=[pltpu.VMEM((tm, D_MODEL), BF16), pltpu.VMEM((tm, D_MODEL), F32)],
        compiler_params=_params(("parallel", "arbitrary")),
        name="swiglu_ffn",
    )(x2d, scale, shift, g_pre, w_ffn_in, w_ffn_in, w_ffn_out, g_post, gate)


def _encoder_layer(x, mod, cos, sin, bias_table, w, decay_logit, ret_gn):
    b, n, d = x.shape
    sh1, sc1, gt1, sh2, sc2, gt2 = [m.reshape(b, 1, d) for m in jnp.split(mod, N_MOD, axis=-1)]
    x2d = x.reshape(b * n, d)
    proj2d = _input_projection(x2d, n, sc1, sh1, w["g_pre_mix"], w["w_in"])
    proj = proj2d.reshape(b, n, IN_WIDTH)
    y_na = _neighbourhood_attention(proj, bias_table)
    fwd = _retention_pass(proj, decay_logit, cos, sin)
    y_ret = _retention_pass(proj, decay_logit, cos, sin, fwd=fwd, gn=ret_gn)
    x1 = _merge(x2d, n, y_na.reshape(b * n, -1), y_ret.reshape(b * n, -1), proj2d,
                w["w_up_na"], w["w_up_ret"], w["w_out"], w["g_post_mix"], gt1)
    x2 = _ffn(x1, n, sc2, sh2, w["g_pre_ffn"], w["w_ffn_in"], w["w_ffn_out"], w["g_post_ffn"], gt2)
    return x2.reshape(b, n, d)


def kernel(x_prompt, x_sample, c_prompt, c_sample, w_mod, b_mod, g_pre_mix, w_in, rpb, ret_decay_logit,
           ret_gn, w_up_na, w_up_ret, w_out, g_post_mix, g_pre_ffn, w_ffn_in, w_ffn_out, g_post_ffn):
    depth = w_mod.shape[0]
    nb = x_prompt.shape[0]
    c_all = jnp.concatenate([c_prompt, c_sample], axis=0)
    c_all = jnp.pad(c_all, ((0, -c_all.shape[0] % 8), (0, 0)))
    cos, sin = _rope_tables(max(x_prompt.shape[1], x_sample.shape[1]))
    y_prompt, y_sample = x_prompt, x_sample
    for l in range(depth):
        w = {"g_pre_mix": g_pre_mix[l].reshape(1, -1), "w_in": w_in[l].astype(BF16),
             "w_up_na": w_up_na[l].astype(BF16), "w_up_ret": w_up_ret[l].astype(BF16),
             "w_out": w_out[l].astype(BF16), "g_post_mix": g_post_mix[l].reshape(1, -1),
             "g_pre_ffn": g_pre_ffn[l].reshape(1, -1), "w_ffn_in": w_ffn_in[l].astype(BF16),
             "w_ffn_out": w_ffn_out[l].astype(BF16), "g_post_ffn": g_post_ffn[l].reshape(1, -1)}
        mod = _modulation(c_all, w_mod[l], b_mod[l])
        bias_table = _na_bias_table(rpb[l])
        y_prompt = _encoder_layer(y_prompt, mod[:nb], cos, sin, bias_table, w, ret_decay_logit[l], ret_gn[l])
        y_sample = _encoder_layer(y_sample, mod[nb:nb + x_sample.shape[0]], cos, sin, bias_table, w, ret_decay_logit[l], ret_gn[l])
    return (y_prompt, y_sample)
```

```python
import functools

import jax
import jax.numpy as jnp
from jax import lax
from jax.experimental import pallas as pl
from jax.experimental.pallas import tpu as pltpu

D_MODEL = 1024
GRID_W = 64
NA_HEADS = 16
NA_HEAD_DIM = 64
NA_KH = 8
NA_KW = 16
RET_HEADS = 4
RET_KEY_DIM = 256
RET_VAL_DIM = 512
ROPE_BASE = 10000.0
FFN_HIDDEN = 2816
IN_WIDTH = 11264
N_MOD = 6
EPS = 1e-6

BF16 = jnp.bfloat16
F32 = jnp.float32

COL_NA_Q, COL_NA_K, COL_NA_V = 0, 1, 2
COL_RET_Q, COL_RET_K = 12, 16
COL_RET_V = 10
COL_RET_GATE, COL_G_NA, COL_G_RET = 7, 9, 10

INPROJ_TM, INPROJ_TN = 1024, 1024
INPROJ_TILE_RET_Q = 3072 // INPROJ_TN
INPROJ_TILE_RET_K = 4096 // INPROJ_TN

NEG_BIAS = -1e30
NA_ROWS_PER_STEP = 8
NA_PAIRS = NA_HEADS // 2
RET_CHUNK = 256
RET_BLOCK = 2048
VMEM_LIMIT = 56 * 1024 * 1024


def _params(semantics, vmem=VMEM_LIMIT):
    return pltpu.CompilerParams(dimension_semantics=semantics, vmem_limit_bytes=vmem)


def _silu(x):
    return x * jax.nn.sigmoid(x)


def _mod_kernel(c_ref, w_ref, b_ref, o_ref):
    s = _silu(c_ref[...]).astype(BF16)
    o_ref[...] = jnp.dot(s, w_ref[...].astype(BF16), preferred_element_type=F32) + b_ref[...]


def _modulation(c, w_mod, b_mod):
    rows = c.shape[0]
    width = w_mod.shape[1]
    tn = 1536
    return pl.pallas_call(
        _mod_kernel,
        grid=(width // tn,),
        in_specs=[pl.BlockSpec((rows, D_MODEL), lambda j: (0, 0)),
                  pl.BlockSpec((D_MODEL, tn), lambda j: (0, j)),
                  pl.BlockSpec((1, tn), lambda j: (0, j))],
        out_specs=pl.BlockSpec((rows, tn), lambda j: (0, j)),
        out_shape=jax.ShapeDtypeStruct((rows, width), F32),
        compiler_params=_params(("arbitrary",)),
        name="modulation",
    )(c, w_mod, b_mod.reshape(1, width))


def _rope_kernel(inv_ref, cos_ref, sin_ref):
    tb = cos_ref.shape[0]
    pos = (pl.program_id(0) * tb + lax.broadcasted_iota(jnp.int32, cos_ref.shape, 0)).astype(F32)
    ang = pos * inv_ref[...]
    cos_ref[...] = jnp.cos(ang)
    sin_ref[...] = jnp.sin(ang)


def _rope_tables(n):
    half = RET_KEY_DIM // 2
    inv = 1.0 / (ROPE_BASE ** (jnp.arange(half, dtype=F32) / half))
    tb = 2048
    spec = pl.BlockSpec((tb, half), lambda i: (i, 0))
    return pl.pallas_call(
        _rope_kernel,
        grid=(n // tb,),
        in_specs=[pl.BlockSpec((1, half), lambda i: (0, 0))],
        out_specs=[spec, spec],
        out_shape=[jax.ShapeDtypeStruct((n, half), F32)] * 2,
        compiler_params=_params(("parallel",)),
        name="rope_tables",
    )(inv.reshape(1, half))


def _modulated_rmsnorm(x, gain, scale, shift):
    y = x * lax.rsqrt(jnp.mean(x * x, axis=-1, keepdims=True) + EPS) * gain
    return y * (1.0 + scale) + shift


def _inproj_kernel(x_ref, sc_ref, sh_ref, g_ref, w_ref, cos_ref, sin_ref, o_ref, h_ref):
    j = pl.program_id(1)

    @pl.when(j == 0)
    def _():
        h = _modulated_rmsnorm(x_ref[...], g_ref[...], sc_ref[0], sh_ref[0])
        h_ref[...] = h.astype(BF16)

    acc = jnp.dot(h_ref[...], w_ref[...], preferred_element_type=F32)
    is_ret_q = j == INPROJ_TILE_RET_Q
    is_ret_k = j == INPROJ_TILE_RET_K
    rotated = jnp.logical_or(is_ret_q, is_ret_k)

    @pl.when(rotated)
    def _():
        cos, sin = cos_ref[...], sin_ref[...]
        scale = jnp.where(is_ret_k, RET_KEY_DIM ** -0.5, 1.0).astype(F32)
        half = RET_KEY_DIM // 2
        for hd in range(RET_HEADS):
            lo = slice(hd * RET_KEY_DIM, hd * RET_KEY_DIM + half)
            hi = slice(hd * RET_KEY_DIM + half, (hd + 1) * RET_KEY_DIM)
            x1, x2 = acc[:, lo], acc[:, hi]
            o_ref[:, lo] = ((x1 * cos - x2 * sin) * scale).astype(o_ref.dtype)
            o_ref[:, hi] = ((x1 * sin + x2 * cos) * scale).astype(o_ref.dtype)

    @pl.when(jnp.logical_not(rotated))
    def _():
        o_ref[...] = acc.astype(o_ref.dtype)


def _input_projection(x2d, n, scale, shift, gain, w_in, cos, sin):
    t = x2d.shape[0]
    tm, tn = INPROJ_TM, INPROJ_TN
    per_seq = n // tm
    vec = pl.BlockSpec((1, 1, D_MODEL), lambda i, j: (i // per_seq, 0, 0))
    rope = pl.BlockSpec((tm, RET_KEY_DIM // 2), lambda i, j: (i % per_seq, 0))
    return pl.pallas_call(
        _inproj_kernel,
        grid=(t // tm, IN_WIDTH // tn),
        in_specs=[pl.BlockSpec((tm, D_MODEL), lambda i, j: (i, 0)),
                  vec, vec,
                  pl.BlockSpec((1, D_MODEL), lambda i, j: (0, 0)),
                  pl.BlockSpec((D_MODEL, tn), lambda i, j: (0, j)),
                  rope, rope],
        out_specs=pl.BlockSpec((tm, tn), lambda i, j: (i, j)),
        out_shape=jax.ShapeDtypeStruct((t, IN_WIDTH), BF16),
        scratch_shapes=[pltpu.VMEM((tm, D_MODEL), BF16)],
        compiler_params=_params(("parallel", "arbitrary")),
        name="input_projection",
    )(x2d, scale, shift, gain, w_in, cos, sin)


def _bias_table_kernel(u_ref, o_ref):
    w = GRID_W
    n_rows = 2 * NA_KH - 1
    c = lax.broadcasted_iota(jnp.int32, (w, 2 * w), 0)
    lane = lax.broadcasted_iota(jnp.int32, (w, 2 * w), 1)
    key_col = lane & (w - 1)
    col_start = jnp.clip(c - NA_KW // 2, 0, w - NA_KW)
    in_window = (key_col >= col_start) & (key_col < col_start + NA_KW)
    first_row = lane < w
    for hl in range(2):
        rolled = []
        for s in range(n_rows):
            u = jnp.broadcast_to(u_ref[0, hl, s:s + 1, :], (w, 2 * w))
            rolled.append((pltpu.roll(u, 0, 1, stride=1, stride_axis=0),
                           pltpu.roll(u, w, 1, stride=1, stride_axis=0)))
        for s in range(n_rows - 1):
            tile = jnp.where(first_row, rolled[s][0], rolled[s + 1][1])
            o_ref[0, s, hl * w:(hl + 1) * w, :] = jnp.where(in_window, tile, NEG_BIAS)


def _na_bias_table(rpb):
    w = GRID_W
    n_rows = 2 * NA_KH - 1
    rpb = rpb.astype(F32)
    gap = jnp.zeros(rpb.shape[:2] + (2 * w - (2 * NA_KW - 1),), F32)
    u = jnp.concatenate([rpb[..., NA_KW - 1:], gap, rpb[..., :NA_KW - 1]], axis=-1)
    u = u.reshape(NA_PAIRS, 2, n_rows, 2 * w)
    return pl.pallas_call(
        _bias_table_kernel,
        grid=(NA_PAIRS,),
        in_specs=[pl.BlockSpec((1, 2, n_rows, 2 * w), lambda p: (p, 0, 0, 0))],
        out_specs=pl.BlockSpec((1, n_rows - 1, 2 * w, 2 * w), lambda p: (p, 0, 0, 0)),
        out_shape=jax.ShapeDtypeStruct((NA_PAIRS, n_rows - 1, 2 * w, 2 * w), F32),
        compiler_params=_params(("parallel",)),
        name="na_bias_table",
    )(u)


def _na_kernel(q_ref, k0_ref, k1_ref, k2_ref, v0_ref, v1_ref, v2_ref, tb_ref, o_ref, kw_ref, vw_ref,
               s0_ref, s1_ref, p0_ref, p1_ref, *, rows_total):
    i = pl.program_id(1)
    s_refs, p_refs = (s0_ref, s1_ref), (p0_ref, p1_ref)
    blk = NA_ROWS_PER_STEP * GRID_W
    for s, (kr, vr) in enumerate(((k0_ref, v0_ref), (k1_ref, v1_ref), (k2_ref, v2_ref))):
        kw_ref[s * blk:(s + 1) * blk, :] = kr[0]
        vw_ref[s * blk:(s + 1) * blk, :] = vr[0]

    w = GRID_W
    row = lax.broadcasted_iota(jnp.int32, (2 * w, 2 * w), 0)
    lane = lax.broadcasted_iota(jnp.int32, (2 * w, 2 * w), 1)
    own_head = (row < w) == (lane < w)
    first_head = lax.broadcasted_iota(jnp.int32, (w, 2 * w), 1) < w
    win = NA_KH * w
    nj = NA_ROWS_PER_STEP

    offs, rels = [], []
    for j in range(nj):
        r = i * nj + j
        rs = jnp.clip(r - NA_KH // 2, 0, rows_total - NA_KH)
        offs.append(pl.multiple_of((rs - (i - 1) * nj) * w, w))
        rels.append(rs - r + (NA_KH - 1))

    for p in range(NA_PAIRS):
        lanes = slice(p * 2 * w, (p + 1) * 2 * w)
        s_ref = s_refs[p % 2]
        p_ref = p_refs[p % 2]
        for j in range(nj):
            qp = q_ref[0, j * w:(j + 1) * w, lanes].astype(F32) * (NA_HEAD_DIM ** -0.5)
            qs = jnp.where(own_head, jnp.concatenate([qp, qp], axis=0), 0.0).astype(BF16)
            kp = kw_ref[pl.ds(offs[j], win), lanes]
            s = lax.dot_general(qs, kp, (((1,), (1,)), ((), ())), preferred_element_type=F32)
            for t in range(NA_KH // 2):
                s_ref[j * 2 * w:(j + 1) * 2 * w, t * 2 * w:(t + 1) * 2 * w] = (
                    s[:, t * 2 * w:(t + 1) * 2 * w] + tb_ref[p, rels[j] + 2 * t])
        s = s_ref[...]
        m = jnp.max(s, axis=1, keepdims=True)
        e = jnp.exp(s - m)
        inv_l = 1.0 / jnp.sum(e, axis=1, keepdims=True)
        p_ref[...] = e.astype(BF16)
        for j in range(nj):
            rows2 = slice(j * 2 * w, (j + 1) * 2 * w)
            vp = vw_ref[pl.ds(offs[j], win), lanes]
            o2 = jnp.dot(p_ref[rows2, :], vp, preferred_element_type=F32) * inv_l[rows2]
            o = jnp.where(first_head, o2[:w], o2[w:])
            o_ref[0, j * w:(j + 1) * w, lanes] = o.astype(o_ref.dtype)


def _neighbourhood_attention(proj, bias_table):
    b, n, _ = proj.shape
    rows = n // GRID_W
    nblk = rows // NA_ROWS_PER_STEP
    blk = NA_ROWS_PER_STEP * GRID_W
    width = NA_HEADS * NA_HEAD_DIM

    def spec(col, shift):
        return pl.BlockSpec((1, blk, width),
                            lambda bi, i: (bi, jnp.clip(i + shift, 0, nblk - 1), col))

    return pl.pallas_call(
        functools.partial(_na_kernel, rows_total=rows),
        grid=(b, nblk),
        in_specs=[spec(COL_NA_Q, 0),
                  spec(COL_NA_K, -1), spec(COL_NA_K, 0), spec(COL_NA_K, 1),
                  spec(COL_NA_V, -1), spec(COL_NA_V, 0), spec(COL_NA_V, 1),
                  pl.BlockSpec(bias_table.shape, lambda bi, i: (0, 0, 0, 0))],
        out_specs=pl.BlockSpec((1, blk, width), lambda bi, i: (bi, i, 0)),
        out_shape=jax.ShapeDtypeStruct((b, n, width), BF16),
        scratch_shapes=[pltpu.VMEM((3 * blk, width), BF16), pltpu.VMEM((3 * blk, width), BF16),
                        pltpu.VMEM((2 * blk, NA_KH * GRID_W), F32), pltpu.VMEM((2 * blk, NA_KH * GRID_W), F32),
                        pltpu.VMEM((2 * blk, NA_KH * GRID_W), BF16), pltpu.VMEM((2 * blk, NA_KH * GRID_W), BF16)],
        compiler_params=_params(("parallel", "arbitrary")),
        name="neighbourhood_attention",
    )(proj, proj, proj, proj, proj, proj, proj, bias_table)


def _ret_kernel(*refs, backward):
    if backward:
        dl_ref, q_ref, k_ref, v_ref, fwd_ref, o_ref, state_ref = refs
    else:
        dl_ref, q_ref, k_ref, v_ref, o_ref, state_ref = refs
    c = RET_CHUNK

    @pl.when(pl.program_id(2) == 0)
    def _():
        state_ref[...] = jnp.zeros_like(state_ref)

    logit = dl_ref[0, 0]
    lg = jnp.minimum(logit, 0.0) - jnp.log(1.0 + jnp.exp(-jnp.abs(logit)))
    row = lax.broadcasted_iota(jnp.int32, (c, c), 0)
    col = lax.broadcasted_iota(jnp.int32, (c, c), 1)
    pos = lax.broadcasted_iota(jnp.int32, (c, 1), 0).astype(F32)
    if backward:
        rel = (col - row).astype(F32)
        mask = rel > 0
        xi = jnp.exp((c - pos) * lg)
        zeta = jnp.exp(pos * lg)
    else:
        rel = (row - col).astype(F32)
        mask = rel >= 0
        xi = jnp.exp((pos + 1.0) * lg)
        zeta = jnp.exp((c - 1.0 - pos) * lg)
    decay = jnp.where(mask, jnp.exp(jnp.where(mask, rel, 0.0) * lg), 0.0)
    chunk_decay = jnp.exp(c * lg)

    nsub = q_ref.shape[1] // c
    order = range(nsub - 1, -1, -1) if backward else range(nsub)
    for u in order:
        sl = slice(u * c, (u + 1) * c)
        qb = q_ref[0, sl, :]
        kb = k_ref[0, sl, :]
        v = v_ref[0, sl, :]
        scores = lax.dot_general(qb, kb, (((1,), (1,)), ((), ())), preferred_element_type=F32) * decay
        intra = jnp.dot(scores.astype(BF16), v, preferred_element_type=F32)
        state = state_ref[...]
        cross = jnp.dot(qb, state.astype(BF16), preferred_element_type=F32) * xi
        kz = (kb.astype(F32) * zeta).astype(BF16)
        update = lax.dot_general(kz, v, (((0,), (0,)), ((), ())), preferred_element_type=F32)
        state_ref[...] = state * chunk_decay + update
        y = intra + cross
        if backward:
            y = y + fwd_ref[0, sl, :].astype(F32)
        o_ref[0, sl, :] = y.astype(o_ref.dtype)


def _retention_pass(proj, decay_logit, fwd=None):
    b, n, _ = proj.shape
    backward = fwd is not None
    nblk = n // RET_BLOCK

    def tok(ci):
        return nblk - 1 - ci if backward else ci

    in_specs = [pl.BlockSpec((1, 1, 1, 1), lambda bi, h, ci: (1 if backward else 0, h, 0, 0)),
                pl.BlockSpec((1, RET_BLOCK, RET_KEY_DIM), lambda bi, h, ci: (bi, tok(ci), COL_RET_Q + h)),
                pl.BlockSpec((1, RET_BLOCK, RET_KEY_DIM), lambda bi, h, ci: (bi, tok(ci), COL_RET_K + h)),
                pl.BlockSpec((1, RET_BLOCK, RET_VAL_DIM), lambda bi, h, ci: (bi, tok(ci), COL_RET_V + h))]
    args = [decay_logit.reshape(2, RET_HEADS, 1, 1), proj, proj, proj]
    out_spec = pl.BlockSpec((1, RET_BLOCK, RET_VAL_DIM), lambda bi, h, ci: (bi, tok(ci), h))
    if backward:
        in_specs.append(out_spec)
        args.append(fwd)
    return pl.pallas_call(
        functools.partial(_ret_kernel, backward=backward),
        grid=(b, RET_HEADS, nblk),
        in_specs=in_specs,
        out_specs=out_spec,
        out_shape=jax.ShapeDtypeStruct((b, n, RET_HEADS * RET_VAL_DIM), BF16),
        scratch_shapes=[pltpu.VMEM((RET_KEY_DIM, RET_VAL_DIM), F32)],
        compiler_params=_params(("parallel", "parallel", "arbitrary")),
        name="retention_bwd" if backward else "retention_fwd",
    )(*args)


def _merge_kernel(x_ref, yna_ref, yret_ref, rg0_ref, rg1_ref, gna_ref, gret_ref, gn_ref,
                  wna_ref, wret_ref, wout_ref, gpost_ref, gt_ref, o_ref):
    a = jnp.dot(yna_ref[...], wna_ref[...], preferred_element_type=F32)
    heads_per_ref = RET_HEADS // 2
    gated = []
    for hd in range(RET_HEADS):
        cols = slice(hd * RET_VAL_DIM, (hd + 1) * RET_VAL_DIM)
        y = yret_ref[:, cols].astype(F32)
        mu = jnp.mean(y, axis=-1, keepdims=True)
        yc = y - mu
        var = jnp.mean(yc * yc, axis=-1, keepdims=True)
        yn = yc * lax.rsqrt(var + EPS) * gn_ref[:, cols]
        rg_ref = rg0_ref if hd < heads_per_ref else rg1_ref
        gcols = slice((hd % heads_per_ref) * RET_VAL_DIM, (hd % heads_per_ref + 1) * RET_VAL_DIM)
        gated.append((_silu(rg_ref[:, gcols].astype(F32)) * yn).astype(BF16))
    r = jnp.dot(jnp.concatenate(gated, axis=1), wret_ref[...], preferred_element_type=F32)
    merged = jax.nn.sigmoid(gna_ref[...].astype(F32)) * a + jax.nn.sigmoid(gret_ref[...].astype(F32)) * r
    out = jnp.dot(merged.astype(BF16), wout_ref[...], preferred_element_type=F32)
    y = out * lax.rsqrt(jnp.mean(out * out, axis=-1, keepdims=True) + EPS) * gpost_ref[...]
    o_ref[...] = x_ref[...] + gt_ref[0] * y


def _merge(x2d, n, y_na, y_ret, proj2d, ret_gn, w_up_na, w_up_ret, w_out, g_post, gate):
    t = x2d.shape[0]
    tm = 512
    per_seq = n // tm
    ret_w = RET_HEADS * RET_VAL_DIM

    def const(shape):
        return pl.BlockSpec(shape, lambda i: (0, 0), pipeline_mode=pl.Buffered(1))

    def proj_cols(col):
        return pl.BlockSpec((tm, D_MODEL), lambda i: (i, col))

    return pl.pallas_call(
        _merge_kernel,
        grid=(t // tm,),
        in_specs=[pl.BlockSpec((tm, D_MODEL), lambda i: (i, 0)),
                  pl.BlockSpec((tm, D_MODEL), lambda i: (i, 0)),
                  pl.BlockSpec((tm, ret_w), lambda i: (i, 0)),
                  proj_cols(COL_RET_GATE), proj_cols(COL_RET_GATE + 1),
                  proj_cols(COL_G_NA), proj_cols(COL_G_RET),
                  const((1, ret_w)),
                  const((D_MODEL, D_MODEL)), const((ret_w, D_MODEL)), const((D_MODEL, D_MODEL)),
                  const((1, D_MODEL)),
                  pl.BlockSpec((1, 1, D_MODEL), lambda i: (i // per_seq, 0, 0))],
        out_specs=pl.BlockSpec((tm, D_MODEL), lambda i: (i, 0)),
        out_shape=jax.ShapeDtypeStruct((t, D_MODEL), F32),
        compiler_params=_params(("parallel",)),
        name="merge_out_projection",
    )(x2d, y_na, y_ret, proj2d, proj2d, proj2d, proj2d, ret_gn.reshape(1, ret_w),
      w_up_na, w_up_ret, w_out, g_post, gate)


def _ffn_kernel(x_ref, sc_ref, sh_ref, gpre_ref, wa_ref, wg_ref, wout_ref, gpost_ref, gt_ref, o_ref,
                h_ref, acc_ref):
    j = pl.program_id(1)

    @pl.when(j == 0)
    def _():
        h = _modulated_rmsnorm(x_ref[...], gpre_ref[...], sc_ref[0], sh_ref[0])
        h_ref[...] = h.astype(BF16)

    h = h_ref[...]
    a = jnp.dot(h, wa_ref[...], preferred_element_type=F32)
    g = jnp.dot(h, wg_ref[...], preferred_element_type=F32)
    part = jnp.dot((_silu(a) * g).astype(BF16), wout_ref[...], preferred_element_type=F32)

    @pl.when(j == 0)
    def _():
        acc_ref[...] = part

    @pl.when(j > 0)
    def _():
        acc_ref[...] += part

    @pl.when(j == pl.num_programs(1) - 1)
    def _():
        f = acc_ref[...]
        y = f * lax.rsqrt(jnp.mean(f * f, axis=-1, keepdims=True) + EPS) * gpost_ref[...]
        o_ref[...] = x_ref[...] + gt_ref[0] * y


def _ffn(x2d, n, scale, shift, g_pre, w_ffn_in, w_ffn_out, g_post, gate):
    t = x2d.shape[0]
    tm = 512
    hc = 1408
    nhc = FFN_HIDDEN // hc
    per_seq = n // tm
    vec = pl.BlockSpec((1, 1, D_MODEL), lambda i, j: (i // per_seq, 0, 0))
    gain = pl.BlockSpec((1, D_MODEL), lambda i, j: (0, 0))
    return pl.pallas_call(
        _ffn_kernel,
        grid=(t // tm, nhc),
        in_specs=[pl.BlockSpec((tm, D_MODEL), lambda i, j: (i, 0)),
                  vec, vec, gain,
                  pl.BlockSpec((D_MODEL, hc), lambda i, j: (0, j)),
                  pl.BlockSpec((D_MODEL, hc), lambda i, j: (0, nhc + j)),
                  pl.BlockSpec((hc, D_MODEL), lambda i, j: (j, 0)),
                  gain, vec],
        out_specs=pl.BlockSpec((tm, D_MODEL), lambda i, j: (i, 0)),
        out_shape=jax.ShapeDtypeStruct((t, D_MODEL), F32),
        scratch_shapes=[pltpu.VMEM((tm, D_MODEL), BF16), pltpu.VMEM((tm, D_MODEL), F32)],
        compiler_params=_params(("parallel", "arbitrary")),
        name="swiglu_ffn",
    )(x2d, scale, shift, g_pre, w_ffn_in, w_ffn_in, w_ffn_out, g_post, gate)


def _encoder_layer(x, mod, cos, sin, bias_table, w, decay_logit, ret_gn):
    b, n, d = x.shape
    sh1, sc1, gt1, sh2, sc2, gt2 = [m.reshape(b, 1, d) for m in jnp.split(mod, N_MOD, axis=-1)]
    x2d = x.reshape(b * n, d)
    proj2d = _input_projection(x2d, n, sc1, sh1, w["g_pre_mix"], w["w_in"], cos, sin)
    proj = proj2d.reshape(b, n, IN_WIDTH)
    y_na = _neighbourhood_attention(proj, bias_table)
    fwd = _retention_pass(proj, decay_logit)
    y_ret = _retention_pass(proj, decay_logit, fwd=fwd)
    x1 = _merge(x2d, n, y_na.reshape(b * n, -1), y_ret.reshape(b * n, -1), proj2d, ret_gn,
                w["w_up_na"], w["w_up_ret"], w["w_out"], w["g_post_mix"], gt1)
    x2 = _ffn(x1, n, sc2, sh2, w["g_pre_ffn"], w["w_ffn_in"], w["w_ffn_out"], w["g_post_ffn"], gt2)
    return x2.reshape(b, n, d)


def kernel(x_prompt, x_sample, c_prompt, c_sample, w_mod, b_mod, g_pre_mix, w_in, rpb, ret_decay_logit,
           ret_gn, w_up_na, w_up_ret, w_out, g_post_mix, g_pre_ffn, w_ffn_in, w_ffn_out, g_post_ffn):
    depth = w_mod.shape[0]
    nb = x_prompt.shape[0]
    c_all = jnp.concatenate([c_prompt, c_sample], axis=0)
    c_all = jnp.pad(c_all, ((0, -c_all.shape[0] % 8), (0, 0)))
    cos, sin = _rope_tables(max(x_prompt.shape[1], x_sample.shape[1]))
    y_prompt, y_sample = x_prompt, x_sample
    for l in range(depth):
        w = {"g_pre_mix": g_pre_mix[l].reshape(1, -1), "w_in": w_in[l].astype(BF16),
             "w_up_na": w_up_na[l].astype(BF16), "w_up_ret": w_up_ret[l].astype(BF16),
             "w_out": w_out[l].astype(BF16), "g_post_mix": g_post_mix[l].reshape(1, -1),
             "g_pre_ffn": g_pre_ffn[l].reshape(1, -1), "w_ffn_in": w_ffn_in[l].astype(BF16),
             "w_ffn_out": w_ffn_out[l].astype(BF16), "g_post_ffn": g_post_ffn[l].reshape(1, -1)}
        mod = _modulation(c_all, w_mod[l], b_mod[l])
        bias_table = _na_bias_table(rpb[l])
        y_prompt = _encoder_layer(y_prompt, mod[:nb], cos, sin, bias_table, w, ret_decay_logit[l], ret_gn[l])
        y_sample = _encoder_layer(y_sample, mod[nb:nb + x_sample.shape[0]], cos, sin, bias_table, w, ret_decay_logit[l], ret_gn[l])
    return (y_prompt, y_sample)
```

```python
import functools

import jax
import jax.numpy as jnp
from jax import lax
from jax.experimental import pallas as pl
from jax.experimental.pallas import tpu as pltpu

D_MODEL = 1024
GRID_W = 64
NA_HEADS = 16
NA_HEAD_DIM = 64
NA_KH = 8
NA_KW = 16
RET_HEADS = 4
RET_KEY_DIM = 256
RET_VAL_DIM = 512
ROPE_BASE = 10000.0
FFN_HIDDEN = 2816
IN_WIDTH = 11264
N_MOD = 6
EPS = 1e-6

BF16 = jnp.bfloat16
F32 = jnp.float32

COL_NA_Q, COL_NA_K, COL_NA_V = 0, 1, 2
COL_RET_Q, COL_RET_K = 12, 16
COL_RET_V = 10
COL_RET_GATE, COL_G_NA, COL_G_RET = 7, 9, 10

INPROJ_TM, INPROJ_TN = 2048, 1024
INPROJ_TILE_RET_Q = 3072 // INPROJ_TN
INPROJ_TILE_RET_K = 4096 // INPROJ_TN

NEG_BIAS = -1e30
NA_ROWS_PER_STEP = 8
NA_WIN_ROWS = NA_ROWS_PER_STEP + NA_KH
NA_PAIRS = NA_HEADS // 2
RET_CHUNK = 256
RET_BLOCK = 2048
VMEM_LIMIT = 56 * 1024 * 1024


def _params(semantics, vmem=VMEM_LIMIT):
    return pltpu.CompilerParams(dimension_semantics=semantics, vmem_limit_bytes=vmem)


def _silu(x):
    return x * jax.nn.sigmoid(x)


def _mod_kernel(c_ref, w_ref, b_ref, o_ref):
    s = _silu(c_ref[...]).astype(BF16)
    o_ref[...] = jnp.dot(s, w_ref[...].astype(BF16), preferred_element_type=F32) + b_ref[...]


def _modulation(c, w_mod, b_mod):
    rows = c.shape[0]
    width = w_mod.shape[1]
    tn = 1536
    return pl.pallas_call(
        _mod_kernel,
        grid=(width // tn,),
        in_specs=[pl.BlockSpec((rows, D_MODEL), lambda j: (0, 0)),
                  pl.BlockSpec((D_MODEL, tn), lambda j: (0, j)),
                  pl.BlockSpec((1, tn), lambda j: (0, j))],
        out_specs=pl.BlockSpec((rows, tn), lambda j: (0, j)),
        out_shape=jax.ShapeDtypeStruct((rows, width), F32),
        compiler_params=_params(("arbitrary",)),
        name="modulation",
    )(c, w_mod, b_mod.reshape(1, width))


def _rope_kernel(inv_ref, cos_ref, sin_ref):
    tb = cos_ref.shape[0]
    pos = (pl.program_id(0) * tb + lax.broadcasted_iota(jnp.int32, cos_ref.shape, 0)).astype(F32)
    ang = pos * inv_ref[...]
    cos_ref[...] = jnp.cos(ang)
    sin_ref[...] = jnp.sin(ang)


def _rope_tables(n):
    half = RET_KEY_DIM // 2
    inv = 1.0 / (ROPE_BASE ** (jnp.arange(half, dtype=F32) / half))
    tb = 2048
    spec = pl.BlockSpec((tb, half), lambda i: (i, 0))
    return pl.pallas_call(
        _rope_kernel,
        grid=(n // tb,),
        in_specs=[pl.BlockSpec((1, half), lambda i: (0, 0))],
        out_specs=[spec, spec],
        out_shape=[jax.ShapeDtypeStruct((n, half), F32)] * 2,
        compiler_params=_params(("parallel",)),
        name="rope_tables",
    )(inv.reshape(1, half))


def _modulated_rmsnorm(x, gain, scale, shift):
    y = x * lax.rsqrt(jnp.mean(x * x, axis=-1, keepdims=True) + EPS) * gain
    return y * (1.0 + scale) + shift


def _inproj_kernel(x_ref, sc_ref, sh_ref, g_ref, w_ref, cos_ref, sin_ref, o_ref, h_ref):
    j = pl.program_id(1)

    @pl.when(j == 0)
    def _():
        h = _modulated_rmsnorm(x_ref[...], g_ref[...], sc_ref[0], sh_ref[0])
        h_ref[...] = h.astype(BF16)

    is_ret_q = j == INPROJ_TILE_RET_Q
    is_ret_k = j == INPROJ_TILE_RET_K
    rotated = jnp.logical_or(is_ret_q, is_ret_k)

    @pl.when(rotated)
    def _():
        scale = jnp.where(is_ret_k, RET_KEY_DIM ** -0.5, 1.0).astype(F32)
        cos, sin = cos_ref[...] * scale, sin_ref[...] * scale
        half = RET_KEY_DIM // 2
        h = h_ref[...]
        for hd in range(INPROJ_TN // RET_KEY_DIM):
            lo = slice(hd * RET_KEY_DIM, hd * RET_KEY_DIM + half)
            hi = slice(hd * RET_KEY_DIM + half, (hd + 1) * RET_KEY_DIM)
            x = jnp.dot(h, w_ref[:, hd * RET_KEY_DIM:(hd + 1) * RET_KEY_DIM], preferred_element_type=F32)
            x1, x2 = x[:, :half], x[:, half:]
            o_ref[:, lo] = (x1 * cos - x2 * sin).astype(o_ref.dtype)
            o_ref[:, hi] = (x1 * sin + x2 * cos).astype(o_ref.dtype)

    @pl.when(jnp.logical_not(rotated))
    def _():
        o_ref[...] = jnp.dot(h_ref[...], w_ref[...], preferred_element_type=F32).astype(o_ref.dtype)


def _input_projection(x2d, n, scale, shift, gain, w_in, cos, sin):
    t = x2d.shape[0]
    tm, tn = INPROJ_TM, INPROJ_TN
    per_seq = n // tm
    vec = pl.BlockSpec((1, 1, D_MODEL), lambda i, j: (i // per_seq, 0, 0))
    rope = pl.BlockSpec((tm, RET_KEY_DIM // 2), lambda i, j: (i % per_seq, 0))
    return pl.pallas_call(
        _inproj_kernel,
        grid=(t // tm, IN_WIDTH // tn),
        in_specs=[pl.BlockSpec((tm, D_MODEL), lambda i, j: (i, 0)),
                  vec, vec,
                  pl.BlockSpec((1, D_MODEL), lambda i, j: (0, 0)),
                  pl.BlockSpec((D_MODEL, tn), lambda i, j: (0, j)),
                  rope, rope],
        out_specs=pl.BlockSpec((tm, tn), lambda i, j: (i, j)),
        out_shape=jax.ShapeDtypeStruct((t, IN_WIDTH), BF16),
        scratch_shapes=[pltpu.VMEM((tm, D_MODEL), BF16)],
        compiler_params=_params(("parallel", "arbitrary")),
        name="input_projection",
    )(x2d, scale, shift, gain, w_in, cos, sin)


def _bias_table_kernel(u_ref, o_ref):
    w = GRID_W
    n_rows = 2 * NA_KH - 1
    c = lax.broadcasted_iota(jnp.int32, (w, 2 * w), 0)
    lane = lax.broadcasted_iota(jnp.int32, (w, 2 * w), 1)
    key_col = lane & (w - 1)
    col_start = jnp.clip(c - NA_KW // 2, 0, w - NA_KW)
    in_window = (key_col >= col_start) & (key_col < col_start + NA_KW)
    first_row = lane < w
    for hl in range(2):
        rolled = []
        for s in range(n_rows):
            u = jnp.broadcast_to(u_ref[0, hl, s:s + 1, :], (w, 2 * w))
            rolled.append((pltpu.roll(u, 0, 1, stride=1, stride_axis=0),
                           pltpu.roll(u, w, 1, stride=1, stride_axis=0)))
        for s in range(n_rows - 1):
            tile = jnp.where(first_row, rolled[s][0], rolled[s + 1][1])
            o_ref[0, s, hl * w:(hl + 1) * w, :] = jnp.where(in_window, tile, NEG_BIAS)


def _na_bias_table(rpb):
    w = GRID_W
    n_rows = 2 * NA_KH - 1
    rpb = rpb.astype(F32)
    gap = jnp.zeros(rpb.shape[:2] + (2 * w - (2 * NA_KW - 1),), F32)
    u = jnp.concatenate([rpb[..., NA_KW - 1:], gap, rpb[..., :NA_KW - 1]], axis=-1)
    u = u.reshape(NA_PAIRS, 2, n_rows, 2 * w)
    return pl.pallas_call(
        _bias_table_kernel,
        grid=(NA_PAIRS,),
        in_specs=[pl.BlockSpec((1, 2, n_rows, 2 * w), lambda p: (p, 0, 0, 0))],
        out_specs=pl.BlockSpec((1, n_rows - 1, 2 * w, 2 * w), lambda p: (p, 0, 0, 0)),
        out_shape=jax.ShapeDtypeStruct((NA_PAIRS, n_rows - 1, 2 * w, 2 * w), F32),
        compiler_params=_params(("parallel",)),
        name="na_bias_table",
    )(u)


def _na_window_start(i, rows_total):
    return jnp.clip(i * NA_ROWS_PER_STEP - NA_KH // 2, 0, rows_total - NA_WIN_ROWS)


def _na_kernel(q_ref, kw_ref, vw_ref, tb_ref, o_ref, s0_ref, s1_ref, p0_ref, p1_ref, *, rows_total):
    i = pl.program_id(1)
    s_refs, p_refs = (s0_ref, s1_ref), (p0_ref, p1_ref)
    w = GRID_W
    row = lax.broadcasted_iota(jnp.int32, (2 * w, 2 * w), 0)
    lane = lax.broadcasted_iota(jnp.int32, (2 * w, 2 * w), 1)
    own_head = jnp.where((row < w) == (lane < w), 1.0, 0.0).astype(BF16)
    first_head = lax.broadcasted_iota(jnp.int32, (w, 2 * w), 1) < w
    win = NA_KH * w
    nj = NA_ROWS_PER_STEP
    ones = jnp.ones((win, 2 * w), BF16)

    ws = _na_window_start(i, rows_total)
    offs, rels = [], []
    for j in range(nj):
        r = i * nj + j
        rs = jnp.clip(r - NA_KH // 2, 0, rows_total - NA_KH)
        offs.append(pl.multiple_of((rs - ws) * w, w))
        rels.append(rs - r + (NA_KH - 1))

    for p in range(NA_PAIRS):
        lanes = slice(p * 2 * w, (p + 1) * 2 * w)
        s_ref = s_refs[p % 2]
        p_ref = p_refs[p % 2]
        stage = [slice(j * 2 * w, (j + 1) * 2 * w) for j in range(nj)]
        for j in range(nj):
            qp = q_ref[0, j * w:(j + 1) * w, lanes]
            qs = jnp.concatenate([qp, qp], axis=0) * own_head
            kp = kw_ref[0, pl.ds(offs[j], win), lanes]
            s = lax.dot_general(qs, kp, (((1,), (1,)), ((), ())), preferred_element_type=F32)
            for t in range(NA_KH // 2):
                s_ref[stage[j], t * 2 * w:(t + 1) * 2 * w] = (
                    s[:, t * 2 * w:(t + 1) * 2 * w] + tb_ref[p, rels[j] + 2 * t])
        for j in range(nj):
            s = s_ref[stage[j], :]
            p_ref[stage[j], :] = jnp.exp(s - jnp.max(s, axis=1, keepdims=True)).astype(BF16)
        for j in range(nj):
            v_aug = jnp.concatenate([vw_ref[0, pl.ds(offs[j], win), lanes], ones], axis=1)
            o2 = jnp.dot(p_ref[stage[j], :], v_aug, preferred_element_type=F32)
            o2 = o2[:, :2 * w] / o2[:, 2 * w:]
            o = jnp.where(first_head, o2[:w], o2[w:])
            o_ref[0, j * w:(j + 1) * w, lanes] = o.astype(o_ref.dtype)


def _neighbourhood_attention(proj, bias_table):
    b, n, _ = proj.shape
    rows = n // GRID_W
    nblk = rows // NA_ROWS_PER_STEP
    blk = NA_ROWS_PER_STEP * GRID_W
    width = NA_HEADS * NA_HEAD_DIM

    def window(col):
        return pl.BlockSpec((pl.Element(1), pl.Element(NA_WIN_ROWS * GRID_W), pl.Element(width)),
                            lambda bi, i: (bi, _na_window_start(i, rows) * GRID_W, col * width))

    return pl.pallas_call(
        functools.partial(_na_kernel, rows_total=rows),
        grid=(b, nblk),
        in_specs=[pl.BlockSpec((1, blk, width), lambda bi, i: (bi, i, COL_NA_Q)),
                  window(COL_NA_K), window(COL_NA_V),
                  pl.BlockSpec(bias_table.shape, lambda bi, i: (0, 0, 0, 0), pipeline_mode=pl.Buffered(1))],
        out_specs=pl.BlockSpec((1, blk, width), lambda bi, i: (bi, i, 0)),
        out_shape=jax.ShapeDtypeStruct((b, n, width), BF16),
        scratch_shapes=[pltpu.VMEM((2 * blk, NA_KH * GRID_W), F32), pltpu.VMEM((2 * blk, NA_KH * GRID_W), F32),
                        pltpu.VMEM((2 * blk, NA_KH * GRID_W), BF16), pltpu.VMEM((2 * blk, NA_KH * GRID_W), BF16)],
        compiler_params=_params(("parallel", "arbitrary")),
        name="neighbourhood_attention",
    )(proj, proj, proj, bias_table)


def _ret_kernel(*refs, backward):
    if backward:
        dl_ref, q_ref, k_ref, v_ref, fwd_ref, o_ref, state_ref = refs
    else:
        dl_ref, q_ref, k_ref, v_ref, o_ref, state_ref = refs
    c = RET_CHUNK

    @pl.when(pl.program_id(2) == 0)
    def _():
        state_ref[...] = jnp.zeros_like(state_ref)

    logit = dl_ref[0, 0]
    lg = jnp.minimum(logit, 0.0) - jnp.log(1.0 + jnp.exp(-jnp.abs(logit)))
    row = lax.broadcasted_iota(jnp.int32, (c, c), 0)
    col = lax.broadcasted_iota(jnp.int32, (c, c), 1)
    pos = lax.broadcasted_iota(jnp.int32, (c, 1), 0).astype(F32)
    if backward:
        rel = (col - row).astype(F32)
        mask = rel > 0
        xi = jnp.exp((c - pos) * lg)
        zeta = jnp.exp(pos * lg)
    else:
        rel = (row - col).astype(F32)
        mask = rel >= 0
        xi = jnp.exp((pos + 1.0) * lg)
        zeta = jnp.exp((c - 1.0 - pos) * lg)
    decay = jnp.where(mask, jnp.exp(jnp.where(mask, rel, 0.0) * lg), 0.0)
    chunk_decay = jnp.exp(c * lg)

    nsub = q_ref.shape[1] // c
    order = range(nsub - 1, -1, -1) if backward else range(nsub)
    for u in order:
        sl = slice(u * c, (u + 1) * c)
        qb = q_ref[0, sl, :]
        kb = k_ref[0, sl, :]
        v = v_ref[0, sl, :]
        scores = lax.dot_general(qb, kb, (((1,), (1,)), ((), ())), preferred_element_type=F32) * decay
        intra = jnp.dot(scores.astype(BF16), v, preferred_element_type=F32)
        state = state_ref[...]
        cross = jnp.dot(qb, state.astype(BF16), preferred_element_type=F32) * xi
        kz = (kb.astype(F32) * zeta).astype(BF16)
        update = lax.dot_general(kz, v, (((0,), (0,)), ((), ())), preferred_element_type=F32)
        state_ref[...] = state * chunk_decay + update
        y = intra + cross
        if backward:
            y = y + fwd_ref[0, sl, :].astype(F32)
        o_ref[0, sl, :] = y.astype(o_ref.dtype)


def _retention_pass(proj, decay_logit, fwd=None):
    b, n, _ = proj.shape
    backward = fwd is not None
    nblk = n // RET_BLOCK

    def tok(ci):
        return nblk - 1 - ci if backward else ci

    in_specs = [pl.BlockSpec((1, 1, 1, 1), lambda bi, h, ci: (1 if backward else 0, h, 0, 0)),
                pl.BlockSpec((1, RET_BLOCK, RET_KEY_DIM), lambda bi, h, ci: (bi, tok(ci), COL_RET_Q + h)),
                pl.BlockSpec((1, RET_BLOCK, RET_KEY_DIM), lambda bi, h, ci: (bi, tok(ci), COL_RET_K + h)),
                pl.BlockSpec((1, RET_BLOCK, RET_VAL_DIM), lambda bi, h, ci: (bi, tok(ci), COL_RET_V + h))]
    args = [decay_logit.reshape(2, RET_HEADS, 1, 1), proj, proj, proj]
    out_spec = pl.BlockSpec((1, RET_BLOCK, RET_VAL_DIM), lambda bi, h, ci: (bi, tok(ci), h))
    if backward:
        in_specs.append(out_spec)
        args.append(fwd)
    return pl.pallas_call(
        functools.partial(_ret_kernel, backward=backward),
        grid=(b, RET_HEADS, nblk),
        in_specs=in_specs,
        out_specs=out_spec,
        out_shape=jax.ShapeDtypeStruct((b, n, RET_HEADS * RET_VAL_DIM), BF16),
        scratch_shapes=[pltpu.VMEM((RET_KEY_DIM, RET_VAL_DIM), F32)],
        compiler_params=_params(("parallel", "parallel", "arbitrary")),
        name="retention_bwd" if backward else "retention_fwd",
    )(*args)


def _merge_kernel(x_ref, yna_ref, yret_ref, rg0_ref, rg1_ref, gna_ref, gret_ref, gn_ref,
                  wna_ref, wret_ref, wout_ref, gpost_ref, gt_ref, o_ref):
    a = jnp.dot(yna_ref[...], wna_ref[...], preferred_element_type=F32)
    heads_per_ref = RET_HEADS // 2
    gated = []
    for hd in range(RET_HEADS):
        cols = slice(hd * RET_VAL_DIM, (hd + 1) * RET_VAL_DIM)
        y = yret_ref[:, cols].astype(F32)
        mu = jnp.mean(y, axis=-1, keepdims=True)
        yc = y - mu
        var = jnp.mean(yc * yc, axis=-1, keepdims=True)
        yn = yc * lax.rsqrt(var + EPS) * gn_ref[:, cols]
        rg_ref = rg0_ref if hd < heads_per_ref else rg1_ref
        gcols = slice((hd % heads_per_ref) * RET_VAL_DIM, (hd % heads_per_ref + 1) * RET_VAL_DIM)
        gated.append((_silu(rg_ref[:, gcols].astype(F32)) * yn).astype(BF16))
    r = jnp.dot(jnp.concatenate(gated, axis=1), wret_ref[...], preferred_element_type=F32)
    merged = jax.nn.sigmoid(gna_ref[...].astype(F32)) * a + jax.nn.sigmoid(gret_ref[...].astype(F32)) * r
    out = jnp.dot(merged.astype(BF16), wout_ref[...], preferred_element_type=F32)
    y = out * lax.rsqrt(jnp.mean(out * out, axis=-1, keepdims=True) + EPS) * gpost_ref[...]
    o_ref[...] = x_ref[...] + gt_ref[0] * y


def _merge(x2d, n, y_na, y_ret, proj2d, ret_gn, w_up_na, w_up_ret, w_out, g_post, gate):
    t = x2d.shape[0]
    tm = 512
    per_seq = n // tm
    ret_w = RET_HEADS * RET_VAL_DIM

    def const(shape):
        return pl.BlockSpec(shape, lambda i: (0, 0), pipeline_mode=pl.Buffered(1))

    def proj_cols(col):
        return pl.BlockSpec((tm, D_MODEL), lambda i: (i, col))

    return pl.pallas_call(
        _merge_kernel,
        grid=(t // tm,),
        in_specs=[pl.BlockSpec((tm, D_MODEL), lambda i: (i, 0)),
                  pl.BlockSpec((tm, D_MODEL), lambda i: (i, 0)),
                  pl.BlockSpec((tm, ret_w), lambda i: (i, 0)),
                  proj_cols(COL_RET_GATE), proj_cols(COL_RET_GATE + 1),
                  proj_cols(COL_G_NA), proj_cols(COL_G_RET),
                  const((1, ret_w)),
                  const((D_MODEL, D_MODEL)), const((ret_w, D_MODEL)), const((D_MODEL, D_MODEL)),
                  const((1, D_MODEL)),
                  pl.BlockSpec((1, 1, D_MODEL), lambda i: (i // per_seq, 0, 0))],
        out_specs=pl.BlockSpec((tm, D_MODEL), lambda i: (i, 0)),
        out_shape=jax.ShapeDtypeStruct((t, D_MODEL), F32),
        compiler_params=_params(("parallel",)),
        name="merge_out_projection",
    )(x2d, y_na, y_ret, proj2d, proj2d, proj2d, proj2d, ret_gn.reshape(1, ret_w),
      w_up_na, w_up_ret, w_out, g_post, gate)


def _ffn_kernel(x_ref, sc_ref, sh_ref, gpre_ref, win_ref, wout_ref, gpost_ref, gt_ref, o_ref):
    x = x_ref[...]
    h = _modulated_rmsnorm(x, gpre_ref[...], sc_ref[0], sh_ref[0]).astype(BF16)
    a = jnp.dot(h, win_ref[:, :FFN_HIDDEN], preferred_element_type=F32)
    g = jnp.dot(h, win_ref[:, FFN_HIDDEN:], preferred_element_type=F32)
    f = jnp.dot((_silu(a) * g).astype(BF16), wout_ref[...], preferred_element_type=F32)
    y = f * lax.rsqrt(jnp.mean(f * f, axis=-1, keepdims=True) + EPS) * gpost_ref[...]
    o_ref[...] = x + gt_ref[0] * y


def _ffn(x2d, n, scale, shift, g_pre, w_ffn_in, w_ffn_out, g_post, gate):
    t = x2d.shape[0]
    tm = 512
    per_seq = n // tm
    vec = pl.BlockSpec((1, 1, D_MODEL), lambda i: (i // per_seq, 0, 0))

    def const(shape):
        return pl.BlockSpec(shape, lambda i: (0, 0), pipeline_mode=pl.Buffered(1))

    return pl.pallas_call(
        _ffn_kernel,
        grid=(t // tm,),
        in_specs=[pl.BlockSpec((tm, D_MODEL), lambda i: (i, 0)),
                  vec, vec, const((1, D_MODEL)),
                  const((D_MODEL, 2 * FFN_HIDDEN)), const((FFN_HIDDEN, D_MODEL)),
                  const((1, D_MODEL)), vec],
        out_specs=pl.BlockSpec((tm, D_MODEL), lambda i: (i, 0)),
        out_shape=jax.ShapeDtypeStruct((t, D_MODEL), F32),
        compiler_params=_params(("parallel",)),
        name="swiglu_ffn",
    )(x2d, scale, shift, g_pre, w_ffn_in, w_ffn_out, g_post, gate)


def _encoder_layer(x, mod, cos, sin, bias_table, w, decay_logit, ret_gn):
    b, n, d = x.shape
    sh1, sc1, gt1, sh2, sc2, gt2 = [m.reshape(b, 1, d) for m in jnp.split(mod, N_MOD, axis=-1)]
    x2d = x.reshape(b * n, d)
    proj2d = _input_projection(x2d, n, sc1, sh1, w["g_pre_mix"], w["w_in"], cos, sin)
    proj = proj2d.reshape(b, n, IN_WIDTH)
    y_na = _neighbourhood_attention(proj, bias_table)
    fwd = _retention_pass(proj, decay_logit)
    y_ret = _retention_pass(proj, decay_logit, fwd=fwd)
    x1 = _merge(x2d, n, y_na.reshape(b * n, -1), y_ret.reshape(b * n, -1), proj2d, ret_gn,
                w["w_up_na"], w["w_up_ret"], w["w_out"], w["g_post_mix"], gt1)
    x2 = _ffn(x1, n, sc2, sh2, w["g_pre_ffn"], w["w_ffn_in"], w["w_ffn_out"], w["g_post_ffn"], gt2)
    return x2.reshape(b, n, d)


def kernel(x_prompt, x_sample, c_prompt, c_sample, w_mod, b_mod, g_pre_mix, w_in, rpb, ret_decay_logit,
           ret_gn, w_up_na, w_up_ret, w_out, g_post_mix, g_pre_ffn, w_ffn_in, w_ffn_out, g_post_ffn):
    depth = w_mod.shape[0]
    nb = x_prompt.shape[0]
    c_all = jnp.concatenate([c_prompt, c_sample], axis=0)
    c_all = jnp.pad(c_all, ((0, -c_all.shape[0] % 8), (0, 0)))
    cos, sin = _rope_tables(max(x_prompt.shape[1], x_sample.shape[1]))
    y_prompt, y_sample = x_prompt, x_sample
    for l in range(depth):
        q_cols = lax.broadcasted_iota(jnp.int32, (1, IN_WIDTH), 1) < NA_HEADS * NA_HEAD_DIM
        col_scale = jnp.where(q_cols, NA_HEAD_DIM ** -0.5, 1.0).astype(F32)
        w = {"g_pre_mix": g_pre_mix[l].reshape(1, -1), "w_in": (w_in[l] * col_scale).astype(BF16),
             "w_up_na": w_up_na[l].astype(BF16), "w_up_ret": w_up_ret[l].astype(BF16),
             "w_out": w_out[l].astype(BF16), "g_post_mix": g_post_mix[l].reshape(1, -1),
             "g_pre_ffn": g_pre_ffn[l].reshape(1, -1), "w_ffn_in": w_ffn_in[l].astype(BF16),
             "w_ffn_out": w_ffn_out[l].astype(BF16), "g_post_ffn": g_post_ffn[l].reshape(1, -1)}
        mod = _modulation(c_all, w_mod[l], b_mod[l])
        bias_table = _na_bias_table(rpb[l])
        y_prompt = _encoder_layer(y_prompt, mod[:nb], cos, sin, bias_table, w, ret_decay_logit[l], ret_gn[l])
        y_sample = _encoder_layer(y_sample, mod[nb:nb + x_sample.shape[0]], cos, sin, bias_table, w, ret_decay_logit[l], ret_gn[l])
    return (y_prompt, y_sample)
```

```python
import functools

import jax
import jax.numpy as jnp
from jax import lax
from jax.experimental import pallas as pl
from jax.experimental.pallas import tpu as pltpu

D_MODEL = 1024
GRID_W = 64
NA_HEADS = 16
NA_HEAD_DIM = 64
NA_KH = 8
NA_KW = 16
RET_HEADS = 4
RET_KEY_DIM = 256
RET_VAL_DIM = 512
ROPE_BASE = 10000.0
FFN_HIDDEN = 2816
IN_WIDTH = 11264
N_MOD = 6
EPS = 1e-6

BF16 = jnp.bfloat16
F32 = jnp.float32

COL_NA_Q, COL_NA_K, COL_NA_V = 0, 1, 2
COL_RET_Q, COL_RET_K = 12, 16
COL_RET_V = 10
COL_RET_GATE, COL_G_NA, COL_G_RET = 7, 9, 10

INPROJ_TM = 512
INPROJ_CHUNK = 1024
RET_Q_START, RET_K_START, RET_V_START = 3072, 4096, 5120

NEG_BIAS = -1e30
NA_ROWS_PER_STEP = 8
NA_WIN_ROWS = NA_ROWS_PER_STEP + NA_KH
NA_PHASE_ROWS = 8
NA_PAIRS = NA_HEADS // 2
NA_GROUP_HEADS = 4
MXU_WIDTH = 256
LOG2_E = 1.4426950408889634
RET_CHUNK = 256
RET_BLOCK = 4096
MERGE_SLABS = 2
FFN_TM = 1024
FFN_SLABS = 4
VMEM_LIMIT = 56 * 1024 * 1024


def _params(semantics):
    return pltpu.CompilerParams(dimension_semantics=semantics, vmem_limit_bytes=VMEM_LIMIT)


def _silu(x):
    return x * jax.nn.sigmoid(x)


def _mod_kernel(c_ref, w_ref, b_ref, o_ref):
    s = _silu(c_ref[...]).astype(BF16)
    o_ref[...] = jnp.dot(s, w_ref[...].astype(BF16), preferred_element_type=F32) + b_ref[...]


def _modulation(c, w_mod, b_mod):
    rows = c.shape[0]
    width = w_mod.shape[1]
    tn = 1536
    return pl.pallas_call(
        _mod_kernel,
        grid=(width // tn,),
        in_specs=[pl.BlockSpec((rows, D_MODEL), lambda j: (0, 0)),
                  pl.BlockSpec((D_MODEL, tn), lambda j: (0, j)),
                  pl.BlockSpec((1, tn), lambda j: (0, j))],
        out_specs=pl.BlockSpec((rows, tn), lambda j: (0, j)),
        out_shape=jax.ShapeDtypeStruct((rows, width), F32),
        compiler_params=_params(("arbitrary",)),
        name="modulation",
    )(c, w_mod, b_mod.reshape(1, width))


def _rope_kernel(inv_ref, cos_ref, sin_ref):
    tb, half = cos_ref.shape
    nhi = tb // half
    inv = inv_ref[...]
    lo = lax.broadcasted_iota(jnp.int32, (half, half), 0).astype(F32)
    hi = ((pl.program_id(0) * nhi + lax.broadcasted_iota(jnp.int32, (nhi, half), 0)) * half).astype(F32)
    cos_lo, sin_lo = jnp.cos(lo * inv), jnp.sin(lo * inv)
    cos_hi, sin_hi = jnp.cos(hi * inv), jnp.sin(hi * inv)
    for r in range(nhi):
        rows = slice(r * half, (r + 1) * half)
        c1, s1 = cos_hi[r:r + 1, :], sin_hi[r:r + 1, :]
        cos_ref[rows, :] = c1 * cos_lo - s1 * sin_lo
        sin_ref[rows, :] = s1 * cos_lo + c1 * sin_lo


def _rope_tables(n):
    half = RET_KEY_DIM // 2
    inv = 1.0 / (ROPE_BASE ** (jnp.arange(half, dtype=F32) / half))
    tb = 2048
    spec = pl.BlockSpec((tb, half), lambda i: (i, 0))
    return pl.pallas_call(
        _rope_kernel,
        grid=(n // tb,),
        in_specs=[pl.BlockSpec((1, half), lambda i: (0, 0))],
        out_specs=[spec, spec],
        out_shape=[jax.ShapeDtypeStruct((n, half), F32)] * 2,
        compiler_params=_params(("parallel",)),
        name="rope_tables",
    )(inv.reshape(1, half))


def _modulated_rmsnorm(x, gain, scale, shift):
    y = x * lax.rsqrt(jnp.mean(x * x, axis=-1, keepdims=True) + EPS) * gain
    return y * (1.0 + scale) + shift


def _inproj_kernel(x_ref, sc_ref, sh_ref, g_ref, w_ref, cos_ref, sin_ref, o_ref):
    h = _modulated_rmsnorm(x_ref[...], g_ref[...], sc_ref[0], sh_ref[0]).astype(BF16)
    half = RET_KEY_DIM // 2
    for c0 in range(0, IN_WIDTH, INPROJ_CHUNK):
        if RET_Q_START <= c0 < RET_V_START:
            scale = RET_KEY_DIM ** -0.5 if c0 >= RET_K_START else 1.0
            cos, sin = cos_ref[...] * scale, sin_ref[...] * scale
            for h0 in range(c0, c0 + INPROJ_CHUNK, RET_KEY_DIM):
                x = jnp.dot(h, w_ref[:, h0:h0 + RET_KEY_DIM], preferred_element_type=F32)
                x1, x2 = x[:, :half], x[:, half:]
                o_ref[:, h0:h0 + half] = (x1 * cos - x2 * sin).astype(o_ref.dtype)
                o_ref[:, h0 + half:h0 + RET_KEY_DIM] = (x1 * sin + x2 * cos).astype(o_ref.dtype)
        else:
            cols = slice(c0, c0 + INPROJ_CHUNK)
            o_ref[:, cols] = jnp.dot(h, w_ref[:, cols], preferred_element_type=F32).astype(o_ref.dtype)


def _input_projection(x2d, n, scale, shift, gain, w_in, cos, sin):
    t = x2d.shape[0]
    tm = INPROJ_TM
    per_seq = n // tm
    vec = pl.BlockSpec((1, 1, D_MODEL), lambda i: (i // per_seq, 0, 0))
    rope = pl.BlockSpec((tm, RET_KEY_DIM // 2), lambda i: (i % per_seq, 0))

    def const(shape):
        return pl.BlockSpec(shape, lambda i: (0, 0), pipeline_mode=pl.Buffered(1))

    return pl.pallas_call(
        _inproj_kernel,
        grid=(t // tm,),
        in_specs=[pl.BlockSpec((tm, D_MODEL), lambda i: (i, 0)),
                  vec, vec,
                  const((1, D_MODEL)), const((D_MODEL, IN_WIDTH)),
                  rope, rope],
        out_specs=pl.BlockSpec((tm, IN_WIDTH), lambda i: (i, 0)),
        out_shape=jax.ShapeDtypeStruct((t, IN_WIDTH), BF16),
        compiler_params=_params(("parallel",)),
        name="input_projection",
    )(x2d, scale, shift, gain, w_in, cos, sin)


def _bias_table_kernel(u_ref, o_ref):
    w = GRID_W
    n_rows = 2 * NA_KH - 1
    c = lax.broadcasted_iota(jnp.int32, (w, 2 * w), 0)
    lane = lax.broadcasted_iota(jnp.int32, (w, 2 * w), 1)
    key_col = lane & (w - 1)
    col_start = jnp.clip(c - NA_KW // 2, 0, w - NA_KW)
    in_window = (key_col >= col_start) & (key_col < col_start + NA_KW)
    first_row = lane < w
    for hl in range(2):
        rolled = []
        for s in range(n_rows):
            u = jnp.broadcast_to(u_ref[0, hl, s:s + 1, :], (w, 2 * w))
            rolled.append((pltpu.roll(u, 0, 1, stride=1, stride_axis=0),
                           pltpu.roll(u, w, 1, stride=1, stride_axis=0)))
        for s in range(n_rows - 1):
            tile = jnp.where(first_row, rolled[s][0], rolled[s + 1][1])
            o_ref[0, s, hl * w:(hl + 1) * w, :] = jnp.where(in_window, tile * LOG2_E, NEG_BIAS)


def _na_bias_table(rpb):
    w = GRID_W
    n_rows = 2 * NA_KH - 1
    rpb = rpb.astype(F32)
    gap = jnp.zeros(rpb.shape[:2] + (2 * w - (2 * NA_KW - 1),), F32)
    u = jnp.concatenate([rpb[..., NA_KW - 1:], gap, rpb[..., :NA_KW - 1]], axis=-1)
    u = u.reshape(NA_PAIRS, 2, n_rows, 2 * w)
    return pl.pallas_call(
        _bias_table_kernel,
        grid=(NA_PAIRS,),
        in_specs=[pl.BlockSpec((1, 2, n_rows, 2 * w), lambda p: (p, 0, 0, 0))],
        out_specs=pl.BlockSpec((1, n_rows - 1, 2 * w, 2 * w), lambda p: (p, 0, 0, 0)),
        out_shape=jax.ShapeDtypeStruct((NA_PAIRS, n_rows - 1, 2 * w, 2 * w), F32),
        compiler_params=_params(("parallel",)),
        name="na_bias_table",
    )(u)


def _na_window_start(i, rows_total):
    return jnp.clip(i * NA_ROWS_PER_STEP - NA_KH // 2, 0, rows_total - NA_WIN_ROWS)


def _na_kernel(q_ref, kw_ref, vw_ref, tb_ref, o_ref, s0_ref, s1_ref, p0_ref, p1_ref, *, rows_total):
    i = pl.program_id(1)
    s_refs, p_refs = (s0_ref, s1_ref), (p0_ref, p1_ref)
    w = GRID_W
    g = NA_GROUP_HEADS
    gw = g * w
    pw = 2 * w
    row_head = lax.broadcasted_iota(jnp.int32, (gw, gw), 0) // w
    lane_head = lax.broadcasted_iota(jnp.int32, (gw, gw), 1) // w
    own_head = jnp.where(row_head == lane_head, 1.0, 0.0).astype(BF16)
    first_head = lax.broadcasted_iota(jnp.int32, (w, pw), 1) < w
    win = NA_KH * w
    nj = NA_ROWS_PER_STEP
    ones = jnp.ones((win, MXU_WIDTH - pw), BF16)

    ws = _na_window_start(i, rows_total)
    offs, rels = [], []
    for j in range(nj):
        r = i * nj + j
        rs = jnp.clip(r - NA_KH // 2, 0, rows_total - NA_KH)
        offs.append(pl.multiple_of((rs - ws) * w, w))
        rels.append(rs - r + (NA_KH - 1))

    stage = [slice(j * gw, (j + 1) * gw) for j in range(nj)]
    phases = [(grp, range(j0, j0 + NA_PHASE_ROWS))
              for grp in range(NA_HEADS // g) for j0 in range(0, nj, NA_PHASE_ROWS)]
    for ph, (grp, js) in enumerate(phases):
        lanes = slice(grp * gw, (grp + 1) * gw)
        s_ref = s_refs[ph % 2]
        p_ref = p_refs[ph % 2]
        for j in js:
            qg = q_ref[0, j * w:(j + 1) * w, lanes]
            qs = jnp.concatenate([qg] * g, axis=0) * own_head
            kg = kw_ref[0, pl.ds(offs[j], win), lanes]
            s = lax.dot_general(qs, kg, (((1,), (1,)), ((), ())), preferred_element_type=F32)
            for pr in range(g // 2):
                for t in range(NA_KH // 2):
                    s_ref[j * gw + pr * 2 * w:j * gw + (pr + 1) * 2 * w, t * 2 * w:(t + 1) * 2 * w] = (
                        s[pr * 2 * w:(pr + 1) * 2 * w, t * 2 * w:(t + 1) * 2 * w]
                        + tb_ref[grp * (g // 2) + pr, rels[j] + 2 * t])
        for j in js:
            s = s_ref[stage[j], :]
            p_ref[stage[j], :] = jnp.exp2(s - jnp.max(s, axis=1, keepdims=True)).astype(BF16)
        for j in js:
            for pr in range(g // 2):
                pair_rows = slice(j * gw + pr * pw, j * gw + (pr + 1) * pw)
                pair_lanes = slice(grp * gw + pr * pw, grp * gw + (pr + 1) * pw)
                v_aug = jnp.concatenate([vw_ref[0, pl.ds(offs[j], win), pair_lanes], ones], axis=1)
                og = jnp.dot(p_ref[pair_rows, :], v_aug, preferred_element_type=F32)
                og = og[:, :pw] / og[:, pw:]
                o = jnp.where(first_head, og[:w], og[w:])
                o_ref[0, j * w:(j + 1) * w, pair_lanes] = o.astype(o_ref.dtype)


def _neighbourhood_attention(proj, bias_table):
    b, n, _ = proj.shape
    rows = n // GRID_W
    nblk = rows // NA_ROWS_PER_STEP
    blk = NA_ROWS_PER_STEP * GRID_W
    width = NA_HEADS * NA_HEAD_DIM
    stage_rows = NA_ROWS_PER_STEP * NA_GROUP_HEADS * GRID_W

    def window(col):
        return pl.BlockSpec((pl.Element(1), pl.Element(NA_WIN_ROWS * GRID_W), pl.Element(width)),
                            lambda bi, i: (bi, _na_window_start(i, rows) * GRID_W, col * width))

    return pl.pallas_call(
        functools.partial(_na_kernel, rows_total=rows),
        grid=(b, nblk),
        in_specs=[pl.BlockSpec((1, blk, width), lambda bi, i: (bi, i, COL_NA_Q)),
                  window(COL_NA_K), window(COL_NA_V),
                  pl.BlockSpec(bias_table.shape, lambda bi, i: (0, 0, 0, 0), pipeline_mode=pl.Buffered(1))],
        out_specs=pl.BlockSpec((1, blk, width), lambda bi, i: (bi, i, 0)),
        out_shape=jax.ShapeDtypeStruct((b, n, width), BF16),
        scratch_shapes=[pltpu.VMEM((stage_rows, NA_KH * GRID_W), F32), pltpu.VMEM((stage_rows, NA_KH * GRID_W), F32),
                        pltpu.VMEM((stage_rows, NA_KH * GRID_W), BF16), pltpu.VMEM((stage_rows, NA_KH * GRID_W), BF16)],
        compiler_params=_params(("parallel", "arbitrary")),
        name="neighbourhood_attention",
    )(proj, proj, proj, bias_table)


def _ret_direction(dl_ref, q_ref, k_ref, v_ref, o_ref, state_ref, fwd_ref, blk, backward):
    c = RET_CHUNK
    logit = dl_ref[1 if backward else 0, 0]
    lg = jnp.minimum(logit, 0.0) - jnp.log1p(jnp.exp(-jnp.abs(logit)))
    row = lax.broadcasted_iota(jnp.int32, (c, c), 0)
    col = lax.broadcasted_iota(jnp.int32, (c, c), 1)
    pos = lax.broadcasted_iota(jnp.int32, (c, 1), 0).astype(F32)
    if backward:
        rel = (col - row).astype(F32)
        mask = rel > 0
        xi = jnp.exp((c - pos) * lg)
        zeta = jnp.exp(pos * lg)
    else:
        rel = (row - col).astype(F32)
        mask = rel >= 0
        xi = jnp.exp((pos + 1.0) * lg)
        zeta = jnp.exp((c - 1.0 - pos) * lg)
    decay = jnp.where(mask, jnp.exp(jnp.where(mask, rel, 0.0) * lg), 0.0)
    chunk_decay = jnp.exp(c * lg)

    nsub = q_ref.shape[1] // c
    order = range(nsub - 1, -1, -1) if backward else range(nsub)
    for u in order:
        sl = slice(u * c, (u + 1) * c)
        qb = q_ref[0, sl, :]
        kb = k_ref[0, sl, :]
        v = v_ref[0, sl, :]
        scores = lax.dot_general(qb, kb, (((1,), (1,)), ((), ())), preferred_element_type=F32) * decay
        intra = jnp.dot(scores.astype(BF16), v, preferred_element_type=F32)
        state = state_ref[...]
        cross = jnp.dot(qb, state.astype(BF16), preferred_element_type=F32) * xi
        kz = (kb.astype(F32) * zeta).astype(BF16)
        update = lax.dot_general(kz, v, (((0,), (0,)), ((), ())), preferred_element_type=F32)
        state_ref[...] = state * chunk_decay + update
        y = intra + cross
        parked = pl.ds(pl.multiple_of(blk * RET_BLOCK + u * c, c), c)
        if backward:
            o_ref[0, sl, :] = (y + fwd_ref[parked, :].astype(F32)).astype(o_ref.dtype)
        else:
            fwd_ref[parked, :] = y.astype(fwd_ref.dtype)


def _ret_kernel(dl_ref, q_ref, k_ref, v_ref, o_ref, state_ref, fwd_ref):
    step = pl.program_id(2)
    nblk = pl.num_programs(2) // 2

    @pl.when(jnp.logical_or(step == 0, step == nblk))
    def _():
        state_ref[...] = jnp.zeros_like(state_ref)

    @pl.when(step < nblk)
    def _():
        _ret_direction(dl_ref, q_ref, k_ref, v_ref, o_ref, state_ref, fwd_ref, step, backward=False)

    @pl.when(step >= nblk)
    def _():
        _ret_direction(dl_ref, q_ref, k_ref, v_ref, o_ref, state_ref, fwd_ref, 2 * nblk - 1 - step,
                       backward=True)


def _retention(proj, decay_logit):
    b, n, _ = proj.shape
    nblk = n // RET_BLOCK

    def tok(step):
        return jnp.where(step < nblk, step, 2 * nblk - 1 - step)

    def out_tok(step):
        return jnp.where(step < nblk, nblk - 1, 2 * nblk - 1 - step)

    return pl.pallas_call(
        _ret_kernel,
        grid=(b, RET_HEADS, 2 * nblk),
        in_specs=[pl.BlockSpec((2, 1, 1, 1), lambda bi, h, st: (0, h, 0, 0)),
                  pl.BlockSpec((1, RET_BLOCK, RET_KEY_DIM), lambda bi, h, st: (bi, tok(st), COL_RET_Q + h)),
                  pl.BlockSpec((1, RET_BLOCK, RET_KEY_DIM), lambda bi, h, st: (bi, tok(st), COL_RET_K + h)),
                  pl.BlockSpec((1, RET_BLOCK, RET_VAL_DIM), lambda bi, h, st: (bi, tok(st), COL_RET_V + h))],
        out_specs=pl.BlockSpec((1, RET_BLOCK, RET_VAL_DIM), lambda bi, h, st: (bi, out_tok(st), h)),
        out_shape=jax.ShapeDtypeStruct((b, n, RET_HEADS * RET_VAL_DIM), BF16),
        scratch_shapes=[pltpu.VMEM((RET_KEY_DIM, RET_VAL_DIM), F32), pltpu.VMEM((n, RET_VAL_DIM), BF16)],
        compiler_params=_params(("parallel", "parallel", "arbitrary")),
        name="retention",
    )(decay_logit.reshape(2, RET_HEADS, 1, 1), proj, proj, proj)


def _merge_kernel(x_ref, yna_ref, yret_ref, rg0_ref, rg1_ref, gna_ref, gret_ref, gn_ref,
                  wna_ref, wret_ref, wout_ref, gpost_ref, gt_ref, o_ref):
    heads_per_ref = RET_HEADS // 2
    tm = x_ref.shape[0]
    for slab in range(MERGE_SLABS):
        rows = slice(slab * tm // MERGE_SLABS, (slab + 1) * tm // MERGE_SLABS)
        a = jnp.dot(yna_ref[rows, :], wna_ref[...], preferred_element_type=F32)
        gated = []
        for hd in range(RET_HEADS):
            cols = slice(hd * RET_VAL_DIM, (hd + 1) * RET_VAL_DIM)
            y = yret_ref[rows, cols].astype(F32)
            mu = jnp.mean(y, axis=-1, keepdims=True)
            yc = y - mu
            var = jnp.mean(yc * yc, axis=-1, keepdims=True)
            yn = yc * lax.rsqrt(var + EPS) * gn_ref[:, cols]
            rg_ref = rg0_ref if hd < heads_per_ref else rg1_ref
            gcols = slice((hd % heads_per_ref) * RET_VAL_DIM, (hd % heads_per_ref + 1) * RET_VAL_DIM)
            gated.append((_silu(rg_ref[rows, gcols].astype(F32)) * yn).astype(BF16))
        r = jnp.dot(jnp.concatenate(gated, axis=1), wret_ref[...], preferred_element_type=F32)
        merged = (jax.nn.sigmoid(gna_ref[rows, :].astype(F32)) * a
                  + jax.nn.sigmoid(gret_ref[rows, :].astype(F32)) * r)
        out = jnp.dot(merged.astype(BF16), wout_ref[...], preferred_element_type=F32)
        y = out * lax.rsqrt(jnp.mean(out * out, axis=-1, keepdims=True) + EPS) * gpost_ref[...]
        o_ref[rows, :] = x_ref[rows, :] + gt_ref[0] * y


def _merge(x2d, n, y_na, y_ret, proj2d, ret_gn, w_up_na, w_up_ret, w_out, g_post, gate):
    t = x2d.shape[0]
    tm = 512
    per_seq = n // tm
    ret_w = RET_HEADS * RET_VAL_DIM

    def const(shape):
        return pl.BlockSpec(shape, lambda i: (0, 0), pipeline_mode=pl.Buffered(1))

    def proj_cols(col):
        return pl.BlockSpec((tm, D_MODEL), lambda i: (i, col))

    return pl.pallas_call(
        _merge_kernel,
        grid=(t // tm,),
        in_specs=[pl.BlockSpec((tm, D_MODEL), lambda i: (i, 0)),
                  pl.BlockSpec((tm, D_MODEL), lambda i: (i, 0)),
                  pl.BlockSpec((tm, ret_w), lambda i: (i, 0)),
                  proj_cols(COL_RET_GATE), proj_cols(COL_RET_GATE + 1),
                  proj_cols(COL_G_NA), proj_cols(COL_G_RET),
                  const((1, ret_w)),
                  const((D_MODEL, D_MODEL)), const((ret_w, D_MODEL)), const((D_MODEL, D_MODEL)),
                  const((1, D_MODEL)),
                  pl.BlockSpec((1, 1, D_MODEL), lambda i: (i // per_seq, 0, 0))],
        out_specs=pl.BlockSpec((tm, D_MODEL), lambda i: (i, 0)),
        out_shape=jax.ShapeDtypeStruct((t, D_MODEL), F32),
        compiler_params=_params(("parallel",)),
        name="merge_out_projection",
    )(x2d, y_na, y_ret, proj2d, proj2d, proj2d, proj2d, ret_gn.reshape(1, ret_w),
      w_up_na, w_up_ret, w_out, g_post, gate)


def _ffn_kernel(x_ref, sc_ref, sh_ref, gpre_ref, win_ref, wout_ref, gpost_ref, gt_ref, o_ref):
    tm = x_ref.shape[0]
    for slab in range(FFN_SLABS):
        rows = slice(slab * tm // FFN_SLABS, (slab + 1) * tm // FFN_SLABS)
        x = x_ref[rows, :]
        h = _modulated_rmsnorm(x, gpre_ref[...], sc_ref[0], sh_ref[0]).astype(BF16)
        a = jnp.dot(h, win_ref[:, :FFN_HIDDEN], preferred_element_type=F32)
        g = jnp.dot(h, win_ref[:, FFN_HIDDEN:], preferred_element_type=F32)
        f = jnp.dot((_silu(a) * g).astype(BF16), wout_ref[...], preferred_element_type=F32)
        y = f * lax.rsqrt(jnp.mean(f * f, axis=-1, keepdims=True) + EPS) * gpost_ref[...]
        o_ref[rows, :] = x + gt_ref[0] * y


def _ffn(x2d, n, scale, shift, g_pre, w_ffn_in, w_ffn_out, g_post, gate):
    t = x2d.shape[0]
    tm = FFN_TM
    per_seq = n // tm
    vec = pl.BlockSpec((1, 1, D_MODEL), lambda i: (i // per_seq, 0, 0))

    def const(shape):
        return pl.BlockSpec(shape, lambda i: (0, 0), pipeline_mode=pl.Buffered(1))

    return pl.pallas_call(
        _ffn_kernel,
        grid=(t // tm,),
        in_specs=[pl.BlockSpec((tm, D_MODEL), lambda i: (i, 0)),
                  vec, vec, const((1, D_MODEL)),
                  const((D_MODEL, 2 * FFN_HIDDEN)), const((FFN_HIDDEN, D_MODEL)),
                  const((1, D_MODEL)), vec],
        out_specs=pl.BlockSpec((tm, D_MODEL), lambda i: (i, 0)),
        out_shape=jax.ShapeDtypeStruct((t, D_MODEL), F32),
        compiler_params=_params(("parallel",)),
        name="swiglu_ffn",
    )(x2d, scale, shift, g_pre, w_ffn_in, w_ffn_out, g_post, gate)


def _encoder_layer(x, mod, cos, sin, bias_table, w, decay_logit, ret_gn):
    b, n, d = x.shape
    sh1, sc1, gt1, sh2, sc2, gt2 = [m.reshape(b, 1, d) for m in jnp.split(mod, N_MOD, axis=-1)]
    x2d = x.reshape(b * n, d)
    proj2d = _input_projection(x2d, n, sc1, sh1, w["g_pre_mix"], w["w_in"], cos, sin)
    proj = proj2d.reshape(b, n, IN_WIDTH)
    y_na = _neighbourhood_attention(proj, bias_table)
    y_ret = _retention(proj, decay_logit)
    x1 = _merge(x2d, n, y_na.reshape(b * n, -1), y_ret.reshape(b * n, -1), proj2d, ret_gn,
                w["w_up_na"], w["w_up_ret"], w["w_out"], w["g_post_mix"], gt1)
    x2 = _ffn(x1, n, sc2, sh2, w["g_pre_ffn"], w["w_ffn_in"], w["w_ffn_out"], w["g_post_ffn"], gt2)
    return x2.reshape(b, n, d)


def kernel(x_prompt, x_sample, c_prompt, c_sample, w_mod, b_mod, g_pre_mix, w_in, rpb, ret_decay_logit,
           ret_gn, w_up_na, w_up_ret, w_out, g_post_mix, g_pre_ffn, w_ffn_in, w_ffn_out, g_post_ffn):
    depth = w_mod.shape[0]
    nb = x_prompt.shape[0]
    c_all = jnp.concatenate([c_prompt, c_sample], axis=0)
    c_all = jnp.pad(c_all, ((0, -c_all.shape[0] % 8), (0, 0)))
    cos, sin = _rope_tables(max(x_prompt.shape[1], x_sample.shape[1]))
    y_prompt, y_sample = x_prompt, x_sample
    for l in range(depth):
        q_cols = lax.broadcasted_iota(jnp.int32, (1, IN_WIDTH), 1) < NA_HEADS * NA_HEAD_DIM
        col_scale = jnp.where(q_cols, NA_HEAD_DIM ** -0.5 * LOG2_E, 1.0).astype(F32)
        w = {"g_pre_mix": g_pre_mix[l].reshape(1, -1), "w_in": (w_in[l] * col_scale).astype(BF16),
             "w_up_na": w_up_na[l].astype(BF16), "w_up_ret": w_up_ret[l].astype(BF16),
             "w_out": w_out[l].astype(BF16), "g_post_mix": g_post_mix[l].reshape(1, -1),
             "g_pre_ffn": g_pre_ffn[l].reshape(1, -1), "w_ffn_in": w_ffn_in[l].astype(BF16),
             "w_ffn_out": w_ffn_out[l].astype(BF16), "g_post_ffn": g_post_ffn[l].reshape(1, -1)}
        mod = _modulation(c_all, w_mod[l], b_mod[l])
        bias_table = _na_bias_table(rpb[l])
        y_prompt = _encoder_layer(y_prompt, mod[:nb], cos, sin, bias_table, w, ret_decay_logit[l], ret_gn[l])
        y_sample = _encoder_layer(y_sample, mod[nb:nb + x_sample.shape[0]], cos, sin, bias_table, w, ret_decay_logit[l], ret_gn[l])
    return (y_prompt, y_sample)
```

```python
import functools

import jax
import jax.numpy as jnp
from jax import lax
from jax.experimental import pallas as pl
from jax.experimental.pallas import tpu as pltpu

D_MODEL = 1024
GRID_W = 64
NA_HEADS = 16
NA_HEAD_DIM = 64
NA_KH = 8
NA_KW = 16
RET_HEADS = 4
RET_KEY_DIM = 256
RET_VAL_DIM = 512
ROPE_BASE = 10000.0
FFN_HIDDEN = 2816
IN_WIDTH = 11264
N_MOD = 6
EPS = 1e-6

BF16 = jnp.bfloat16
F32 = jnp.float32

COL_NA_Q, COL_NA_K, COL_NA_V = 0, 1, 2
COL_RET_Q, COL_RET_K = 12, 16
COL_RET_V = 10
COL_RET_GATE, COL_G_NA, COL_G_RET = 7, 9, 10

INPROJ_TM = 512
INPROJ_CHUNK = 1024
RET_Q_START, RET_K_START, RET_V_START = 3072, 4096, 5120

NEG_BIAS = -1e30
NA_ROWS_PER_STEP = 8
NA_WIN_ROWS = NA_ROWS_PER_STEP + NA_KH
NA_PHASE_ROWS = 8
NA_PAIRS = NA_HEADS // 2
NA_GROUP_HEADS = 4
MXU_WIDTH = 256
LOG2_E = 1.4426950408889634
RET_CHUNK = 256
RET_BLOCK = 4096
MERGE_SLABS = 2
FFN_TM = 1024
FFN_SLABS = 4
VMEM_LIMIT = 56 * 1024 * 1024


def _params(semantics):
    return pltpu.CompilerParams(dimension_semantics=semantics, vmem_limit_bytes=VMEM_LIMIT)


def _silu(x):
    return x * jax.nn.sigmoid(x)


def _mod_kernel(c_ref, w_ref, b_ref, o_ref):
    s = _silu(c_ref[...]).astype(BF16)
    o_ref[...] = jnp.dot(s, w_ref[...].astype(BF16), preferred_element_type=F32) + b_ref[...]


def _modulation(c, w_mod, b_mod):
    rows = c.shape[0]
    width = w_mod.shape[1]
    tn = 1536
    return pl.pallas_call(
        _mod_kernel,
        grid=(width // tn,),
        in_specs=[pl.BlockSpec((rows, D_MODEL), lambda j: (0, 0)),
                  pl.BlockSpec((D_MODEL, tn), lambda j: (0, j)),
                  pl.BlockSpec((1, tn), lambda j: (0, j))],
        out_specs=pl.BlockSpec((rows, tn), lambda j: (0, j)),
        out_shape=jax.ShapeDtypeStruct((rows, width), F32),
        compiler_params=_params(("arbitrary",)),
        name="modulation",
    )(c, w_mod, b_mod.reshape(1, width))


def _rope_kernel(inv_ref, cos_ref, sin_ref):
    tb, half = cos_ref.shape
    nhi = tb // half
    inv = inv_ref[...]
    lo = lax.broadcasted_iota(jnp.int32, (half, half), 0).astype(F32)
    hi = ((pl.program_id(0) * nhi + lax.broadcasted_iota(jnp.int32, (nhi, half), 0)) * half).astype(F32)
    cos_lo, sin_lo = jnp.cos(lo * inv), jnp.sin(lo * inv)
    cos_hi, sin_hi = jnp.cos(hi * inv), jnp.sin(hi * inv)
    for r in range(nhi):
        rows = slice(r * half, (r + 1) * half)
        c1, s1 = cos_hi[r:r + 1, :], sin_hi[r:r + 1, :]
        cos_ref[rows, :] = c1 * cos_lo - s1 * sin_lo
        sin_ref[rows, :] = s1 * cos_lo + c1 * sin_lo


def _rope_tables(n):
    half = RET_KEY_DIM // 2
    inv = 1.0 / (ROPE_BASE ** (jnp.arange(half, dtype=F32) / half))
    tb = 2048
    spec = pl.BlockSpec((tb, half), lambda i: (i, 0))
    return pl.pallas_call(
        _rope_kernel,
        grid=(n // tb,),
        in_specs=[pl.BlockSpec((1, half), lambda i: (0, 0))],
        out_specs=[spec, spec],
        out_shape=[jax.ShapeDtypeStruct((n, half), F32)] * 2,
        compiler_params=_params(("parallel",)),
        name="rope_tables",
    )(inv.reshape(1, half))


def _modulated_rmsnorm(x, gain, scale, shift):
    y = x * lax.rsqrt(jnp.mean(x * x, axis=-1, keepdims=True) + EPS) * gain
    return y * (1.0 + scale) + shift


def _inproj_kernel(x_ref, sc_ref, sh_ref, g_ref, w_ref, cos_ref, sin_ref, o_ref):
    h = _modulated_rmsnorm(x_ref[...], g_ref[...], sc_ref[0], sh_ref[0]).astype(BF16)
    half = RET_KEY_DIM // 2
    for c0 in range(0, IN_WIDTH, INPROJ_CHUNK):
        if RET_Q_START <= c0 < RET_V_START:
            scale = RET_KEY_DIM ** -0.5 if c0 >= RET_K_START else 1.0
            cos, sin = cos_ref[...] * scale, sin_ref[...] * scale
            for h0 in range(c0, c0 + INPROJ_CHUNK, RET_KEY_DIM):
                x = jnp.dot(h, w_ref[:, h0:h0 + RET_KEY_DIM], preferred_element_type=F32)
                x1, x2 = x[:, :half], x[:, half:]
                o_ref[:, h0:h0 + half] = (x1 * cos - x2 * sin).astype(o_ref.dtype)
                o_ref[:, h0 + half:h0 + RET_KEY_DIM] = (x1 * sin + x2 * cos).astype(o_ref.dtype)
        else:
            cols = slice(c0, c0 + INPROJ_CHUNK)
            o_ref[:, cols] = jnp.dot(h, w_ref[:, cols], preferred_element_type=F32).astype(o_ref.dtype)


def _input_projection(x2d, n, scale, shift, gain, w_in, cos, sin):
    t = x2d.shape[0]
    tm = INPROJ_TM
    per_seq = n // tm
    vec = pl.BlockSpec((1, 1, D_MODEL), lambda i: (i // per_seq, 0, 0))
    rope = pl.BlockSpec((tm, RET_KEY_DIM // 2), lambda i: (i % per_seq, 0))

    def const(shape):
        return pl.BlockSpec(shape, lambda i: (0, 0), pipeline_mode=pl.Buffered(1))

    return pl.pallas_call(
        _inproj_kernel,
        grid=(t // tm,),
        in_specs=[pl.BlockSpec((tm, D_MODEL), lambda i: (i, 0)),
                  vec, vec,
                  const((1, D_MODEL)), const((D_MODEL, IN_WIDTH)),
                  rope, rope],
        out_specs=pl.BlockSpec((tm, IN_WIDTH), lambda i: (i, 0)),
        out_shape=jax.ShapeDtypeStruct((t, IN_WIDTH), BF16),
        compiler_params=_params(("parallel",)),
        name="input_projection",
    )(x2d, scale, shift, gain, w_in, cos, sin)


def _bias_table_kernel(u_ref, o_ref):
    w = GRID_W
    n_rows = 2 * NA_KH - 1
    c = lax.broadcasted_iota(jnp.int32, (w, 2 * w), 0)
    lane = lax.broadcasted_iota(jnp.int32, (w, 2 * w), 1)
    key_col = lane & (w - 1)
    col_start = jnp.clip(c - NA_KW // 2, 0, w - NA_KW)
    in_window = (key_col >= col_start) & (key_col < col_start + NA_KW)
    first_row = lane < w
    for hl in range(2):
        rolled = []
        for s in range(n_rows):
            u = jnp.broadcast_to(u_ref[0, hl, s:s + 1, :], (w, 2 * w))
            rolled.append((pltpu.roll(u, 0, 1, stride=1, stride_axis=0),
                           pltpu.roll(u, w, 1, stride=1, stride_axis=0)))
        for s in range(n_rows - 1):
            tile = jnp.where(first_row, rolled[s][0], rolled[s + 1][1])
            o_ref[0, s, hl * w:(hl + 1) * w, :] = jnp.where(in_window, tile * LOG2_E, NEG_BIAS)


def _na_bias_table(rpb):
    w = GRID_W
    n_rows = 2 * NA_KH - 1
    rpb = rpb.astype(F32)
    gap = jnp.zeros(rpb.shape[:2] + (2 * w - (2 * NA_KW - 1),), F32)
    u = jnp.concatenate([rpb[..., NA_KW - 1:], gap, rpb[..., :NA_KW - 1]], axis=-1)
    u = u.reshape(NA_PAIRS, 2, n_rows, 2 * w)
    return pl.pallas_call(
        _bias_table_kernel,
        grid=(NA_PAIRS,),
        in_specs=[pl.BlockSpec((1, 2, n_rows, 2 * w), lambda p: (p, 0, 0, 0))],
        out_specs=pl.BlockSpec((1, n_rows - 1, 2 * w, 2 * w), lambda p: (p, 0, 0, 0)),
        out_shape=jax.ShapeDtypeStruct((NA_PAIRS, n_rows - 1, 2 * w, 2 * w), F32),
        compiler_params=_params(("parallel",)),
        name="na_bias_table",
    )(u)


def _na_window_start(i, rows_total):
    return jnp.clip(i * NA_ROWS_PER_STEP - NA_KH // 2, 0, rows_total - NA_WIN_ROWS)


def _na_kernel(q_ref, kw_ref, vw_ref, tb_ref, o_ref, s0_ref, s1_ref, p0_ref, p1_ref, *, rows_total,
               phase_rows):
    i = pl.program_id(1)
    s_refs, p_refs = (s0_ref, s1_ref), (p0_ref, p1_ref)
    w = GRID_W
    g = NA_GROUP_HEADS
    gw = g * w
    pw = 2 * w
    row_head = lax.broadcasted_iota(jnp.int32, (gw, gw), 0) // w
    lane_head = lax.broadcasted_iota(jnp.int32, (gw, gw), 1) // w
    own_head = jnp.where(row_head == lane_head, 1.0, 0.0).astype(BF16)
    first_head = lax.broadcasted_iota(jnp.int32, (w, pw), 1) < w
    win = NA_KH * w
    nj = NA_ROWS_PER_STEP
    ones = jnp.ones((win, MXU_WIDTH - pw), BF16)

    ws = _na_window_start(i, rows_total)
    offs, rels = [], []
    for j in range(nj):
        r = i * nj + j
        rs = jnp.clip(r - NA_KH // 2, 0, rows_total - NA_KH)
        offs.append(pl.multiple_of((rs - ws) * w, w))
        rels.append(rs - r + (NA_KH - 1))

    stage = [slice(j * gw, (j + 1) * gw) for j in range(nj)]
    phases = [(grp, range(j0, j0 + phase_rows))
              for grp in range(NA_HEADS // g) for j0 in range(0, nj, phase_rows)]
    for ph, (grp, js) in enumerate(phases):
        lanes = slice(grp * gw, (grp + 1) * gw)
        s_ref = s_refs[ph % 2]
        p_ref = p_refs[ph % 2]
        for j in js:
            qg = q_ref[0, j * w:(j + 1) * w, lanes]
            qs = jnp.concatenate([qg] * g, axis=0) * own_head
            kg = kw_ref[0, pl.ds(offs[j], win), lanes]
            s = lax.dot_general(qs, kg, (((1,), (1,)), ((), ())), preferred_element_type=F32)
            for pr in range(g // 2):
                for t in range(NA_KH // 2):
                    s_ref[j * gw + pr * 2 * w:j * gw + (pr + 1) * 2 * w, t * 2 * w:(t + 1) * 2 * w] = (
                        s[pr * 2 * w:(pr + 1) * 2 * w, t * 2 * w:(t + 1) * 2 * w]
                        + tb_ref[grp * (g // 2) + pr, rels[j] + 2 * t])
        for j in js:
            s = s_ref[stage[j], :]
            p_ref[stage[j], :] = jnp.exp2(s - jnp.max(s, axis=1, keepdims=True)).astype(BF16)
        for j in js:
            for pr in range(g // 2):
                pair_rows = slice(j * gw + pr * pw, j * gw + (pr + 1) * pw)
                pair_lanes = slice(grp * gw + pr * pw, grp * gw + (pr + 1) * pw)
                v_aug = jnp.concatenate([vw_ref[0, pl.ds(offs[j], win), pair_lanes], ones], axis=1)
                og = jnp.dot(p_ref[pair_rows, :], v_aug, preferred_element_type=F32)
                og = og[:, :pw] / og[:, pw:]
                o = jnp.where(first_head, og[:w], og[w:])
                o_ref[0, j * w:(j + 1) * w, pair_lanes] = o.astype(o_ref.dtype)


def _neighbourhood_attention(proj, bias_table, phase_rows=NA_PHASE_ROWS):
    b, n, _ = proj.shape
    rows = n // GRID_W
    nblk = rows // NA_ROWS_PER_STEP
    blk = NA_ROWS_PER_STEP * GRID_W
    width = NA_HEADS * NA_HEAD_DIM
    stage_rows = NA_ROWS_PER_STEP * NA_GROUP_HEADS * GRID_W

    def window(col):
        return pl.BlockSpec((pl.Element(1), pl.Element(NA_WIN_ROWS * GRID_W), pl.Element(width)),
                            lambda bi, i: (bi, _na_window_start(i, rows) * GRID_W, col * width))

    return pl.pallas_call(
        functools.partial(_na_kernel, rows_total=rows, phase_rows=phase_rows),
        grid=(b, nblk),
        in_specs=[pl.BlockSpec((1, blk, width), lambda bi, i: (bi, i, COL_NA_Q)),
                  window(COL_NA_K), window(COL_NA_V),
                  pl.BlockSpec(bias_table.shape, lambda bi, i: (0, 0, 0, 0), pipeline_mode=pl.Buffered(1))],
        out_specs=pl.BlockSpec((1, blk, width), lambda bi, i: (bi, i, 0)),
        out_shape=jax.ShapeDtypeStruct((b, n, width), BF16),
        scratch_shapes=[pltpu.VMEM((stage_rows, NA_KH * GRID_W), F32), pltpu.VMEM((stage_rows, NA_KH * GRID_W), F32),
                        pltpu.VMEM((stage_rows, NA_KH * GRID_W), BF16), pltpu.VMEM((stage_rows, NA_KH * GRID_W), BF16)],
        compiler_params=_params(("parallel", "arbitrary")),
        name="neighbourhood_attention",
    )(proj, proj, proj, bias_table)


def _ret_direction(dl_ref, q_ref, k_ref, v_ref, o_ref, state_ref, fwd_ref, blk, backward):
    c = RET_CHUNK
    logit = dl_ref[1 if backward else 0, 0]
    lg = jnp.minimum(logit, 0.0) - jnp.log1p(jnp.exp(-jnp.abs(logit)))
    row = lax.broadcasted_iota(jnp.int32, (c, c), 0)
    col = lax.broadcasted_iota(jnp.int32, (c, c), 1)
    pos = lax.broadcasted_iota(jnp.int32, (c, 1), 0).astype(F32)
    if backward:
        rel = (col - row).astype(F32)
        mask = rel > 0
        xi = jnp.exp((c - pos) * lg)
        zeta = jnp.exp(pos * lg)
    else:
        rel = (row - col).astype(F32)
        mask = rel >= 0
        xi = jnp.exp((pos + 1.0) * lg)
        zeta = jnp.exp((c - 1.0 - pos) * lg)
    decay = jnp.where(mask, jnp.exp(jnp.where(mask, rel, 0.0) * lg), 0.0)
    chunk_decay = jnp.exp(c * lg)

    nsub = q_ref.shape[1] // c
    order = range(nsub - 1, -1, -1) if backward else range(nsub)
    for u in order:
        sl = slice(u * c, (u + 1) * c)
        qb = q_ref[0, sl, :]
        kb = k_ref[0, sl, :]
        v = v_ref[0, sl, :]
        scores = lax.dot_general(qb, kb, (((1,), (1,)), ((), ())), preferred_element_type=F32) * decay
        intra = jnp.dot(scores.astype(BF16), v, preferred_element_type=F32)
        state = state_ref[...]
        cross = jnp.dot(qb, state.astype(BF16), preferred_element_type=F32) * xi
        kz = (kb.astype(F32) * zeta).astype(BF16)
        update = lax.dot_general(kz, v, (((0,), (0,)), ((), ())), preferred_element_type=F32)
        state_ref[...] = state * chunk_decay + update
        y = intra + cross
        parked = pl.ds(pl.multiple_of(blk * RET_BLOCK + u * c, c), c)
        if backward:
            o_ref[0, sl, :] = (y + fwd_ref[parked, :].astype(F32)).astype(o_ref.dtype)
        else:
            fwd_ref[parked, :] = y.astype(fwd_ref.dtype)


def _ret_kernel(dl_ref, q_ref, k_ref, v_ref, o_ref, state_ref, fwd_ref):
    step = pl.program_id(2)
    nblk = pl.num_programs(2) // 2

    @pl.when(jnp.logical_or(step == 0, step == nblk))
    def _():
        state_ref[...] = jnp.zeros_like(state_ref)

    @pl.when(step < nblk)
    def _():
        _ret_direction(dl_ref, q_ref, k_ref, v_ref, o_ref, state_ref, fwd_ref, step, backward=False)

    @pl.when(step >= nblk)
    def _():
        _ret_direction(dl_ref, q_ref, k_ref, v_ref, o_ref, state_ref, fwd_ref, 2 * nblk - 1 - step,
                       backward=True)


def _retention(proj, decay_logit):
    b, n, _ = proj.shape
    nblk = n // RET_BLOCK

    def tok(step):
        return jnp.where(step < nblk, step, 2 * nblk - 1 - step)

    def out_tok(step):
        return jnp.where(step < nblk, nblk - 1, 2 * nblk - 1 - step)

    return pl.pallas_call(
        _ret_kernel,
        grid=(b, RET_HEADS, 2 * nblk),
        in_specs=[pl.BlockSpec((2, 1, 1, 1), lambda bi, h, st: (0, h, 0, 0)),
                  pl.BlockSpec((1, RET_BLOCK, RET_KEY_DIM), lambda bi, h, st: (bi, tok(st), COL_RET_Q + h)),
                  pl.BlockSpec((1, RET_BLOCK, RET_KEY_DIM), lambda bi, h, st: (bi, tok(st), COL_RET_K + h)),
                  pl.BlockSpec((1, RET_BLOCK, RET_VAL_DIM), lambda bi, h, st: (bi, tok(st), COL_RET_V + h))],
        out_specs=pl.BlockSpec((1, RET_BLOCK, RET_VAL_DIM), lambda bi, h, st: (bi, out_tok(st), h)),
        out_shape=jax.ShapeDtypeStruct((b, n, RET_HEADS * RET_VAL_DIM), BF16),
        scratch_shapes=[pltpu.VMEM((RET_KEY_DIM, RET_VAL_DIM), F32), pltpu.VMEM((n, RET_VAL_DIM), BF16)],
        compiler_params=_params(("parallel", "parallel", "arbitrary")),
        name="retention",
    )(decay_logit.reshape(2, RET_HEADS, 1, 1), proj, proj, proj)


def _merge_kernel(x_ref, yna_ref, yret_ref, rg0_ref, rg1_ref, gna_ref, gret_ref, gn_ref,
                  wna_ref, wret_ref, wout_ref, gpost_ref, gt_ref, o_ref):
    heads_per_ref = RET_HEADS // 2
    tm = x_ref.shape[0]
    for slab in range(MERGE_SLABS):
        rows = slice(slab * tm // MERGE_SLABS, (slab + 1) * tm // MERGE_SLABS)
        a = jnp.dot(yna_ref[rows, :], wna_ref[...], preferred_element_type=F32)
        gated = []
        for hd in range(RET_HEADS):
            cols = slice(hd * RET_VAL_DIM, (hd + 1) * RET_VAL_DIM)
            y = yret_ref[rows, cols].astype(F32)
            mu = jnp.mean(y, axis=-1, keepdims=True)
            yc = y - mu
            var = jnp.mean(yc * yc, axis=-1, keepdims=True)
            yn = yc * lax.rsqrt(var + EPS) * gn_ref[:, cols]
            rg_ref = rg0_ref if hd < heads_per_ref else rg1_ref
            gcols = slice((hd % heads_per_ref) * RET_VAL_DIM, (hd % heads_per_ref + 1) * RET_VAL_DIM)
            gated.append((_silu(rg_ref[rows, gcols].astype(F32)) * yn).astype(BF16))
        r = jnp.dot(jnp.concatenate(gated, axis=1), wret_ref[...], preferred_element_type=F32)
        merged = (jax.nn.sigmoid(gna_ref[rows, :].astype(F32)) * a
                  + jax.nn.sigmoid(gret_ref[rows, :].astype(F32)) * r)
        out = jnp.dot(merged.astype(BF16), wout_ref[...], preferred_element_type=F32)
        y = out * lax.rsqrt(jnp.mean(out * out, axis=-1, keepdims=True) + EPS) * gpost_ref[...]
        o_ref[rows, :] = x_ref[rows, :] + gt_ref[0] * y


def _merge(x2d, n, y_na, y_ret, proj2d, ret_gn, w_up_na, w_up_ret, w_out, g_post, gate):
    t = x2d.shape[0]
    tm = 512
    per_seq = n // tm
    ret_w = RET_HEADS * RET_VAL_DIM

    def const(shape):
        return pl.BlockSpec(shape, lambda i: (0, 0), pipeline_mode=pl.Buffered(1))

    def proj_cols(col):
        return pl.BlockSpec((tm, D_MODEL), lambda i: (i, col))

    return pl.pallas_call(
        _merge_kernel,
        grid=(t // tm,),
        in_specs=[pl.BlockSpec((tm, D_MODEL), lambda i: (i, 0)),
                  pl.BlockSpec((tm, D_MODEL), lambda i: (i, 0)),
                  pl.BlockSpec((tm, ret_w), lambda i: (i, 0)),
                  proj_cols(COL_RET_GATE), proj_cols(COL_RET_GATE + 1),
                  proj_cols(COL_G_NA), proj_cols(COL_G_RET),
                  const((1, ret_w)),
                  const((D_MODEL, D_MODEL)), const((ret_w, D_MODEL)), const((D_MODEL, D_MODEL)),
                  const((1, D_MODEL)),
                  pl.BlockSpec((1, 1, D_MODEL), lambda i: (i // per_seq, 0, 0))],
        out_specs=pl.BlockSpec((tm, D_MODEL), lambda i: (i, 0)),
        out_shape=jax.ShapeDtypeStruct((t, D_MODEL), F32),
        compiler_params=_params(("parallel",)),
        name="merge_out_projection",
    )(x2d, y_na, y_ret, proj2d, proj2d, proj2d, proj2d, ret_gn.reshape(1, ret_w),
      w_up_na, w_up_ret, w_out, g_post, gate)


def _ffn_kernel(x_ref, sc_ref, sh_ref, gpre_ref, win_ref, wout_ref, gpost_ref, gt_ref, o_ref):
    tm = x_ref.shape[0]
    for slab in range(FFN_SLABS):
        rows = slice(slab * tm // FFN_SLABS, (slab + 1) * tm // FFN_SLABS)
        x = x_ref[rows, :]
        h = _modulated_rmsnorm(x, gpre_ref[...], sc_ref[0], sh_ref[0]).astype(BF16)
        a = jnp.dot(h, win_ref[:, :FFN_HIDDEN], preferred_element_type=F32)
        g = jnp.dot(h, win_ref[:, FFN_HIDDEN:], preferred_element_type=F32)
        f = jnp.dot((_silu(a) * g).astype(BF16), wout_ref[...], preferred_element_type=F32)
        y = f * lax.rsqrt(jnp.mean(f * f, axis=-1, keepdims=True) + EPS) * gpost_ref[...]
        o_ref[rows, :] = x + gt_ref[0] * y


def _ffn(x2d, n, scale, shift, g_pre, w_ffn_in, w_ffn_out, g_post, gate):
    t = x2d.shape[0]
    tm = FFN_TM
    per_seq = n // tm
    vec = pl.BlockSpec((1, 1, D_MODEL), lambda i: (i // per_seq, 0, 0))

    def const(shape):
        return pl.BlockSpec(shape, lambda i: (0, 0), pipeline_mode=pl.Buffered(1))

    return pl.pallas_call(
        _ffn_kernel,
        grid=(t // tm,),
        in_specs=[pl.BlockSpec((tm, D_MODEL), lambda i: (i, 0)),
                  vec, vec, const((1, D_MODEL)),
                  const((D_MODEL, 2 * FFN_HIDDEN)), const((FFN_HIDDEN, D_MODEL)),
                  const((1, D_MODEL)), vec],
        out_specs=pl.BlockSpec((tm, D_MODEL), lambda i: (i, 0)),
        out_shape=jax.ShapeDtypeStruct((t, D_MODEL), F32),
        compiler_params=_params(("parallel",)),
        name="swiglu_ffn",
    )(x2d, scale, shift, g_pre, w_ffn_in, w_ffn_out, g_post, gate)


def _encoder_layer(x, mod, cos, sin, bias_table, w, decay_logit, ret_gn):
    b, n, d = x.shape
    sh1, sc1, gt1, sh2, sc2, gt2 = [m.reshape(b, 1, d) for m in jnp.split(mod, N_MOD, axis=-1)]
    x2d = x.reshape(b * n, d)
    proj2d = _input_projection(x2d, n, sc1, sh1, w["g_pre_mix"], w["w_in"], cos, sin)
    proj = proj2d.reshape(b, n, IN_WIDTH)
    y_na = _neighbourhood_attention(proj, bias_table, phase_rows=4 if b == 1 else NA_PHASE_ROWS)
    y_ret = _retention(proj, decay_logit)
    x1 = _merge(x2d, n, y_na.reshape(b * n, -1), y_ret.reshape(b * n, -1), proj2d, ret_gn,
                w["w_up_na"], w["w_up_ret"], w["w_out"], w["g_post_mix"], gt1)
    x2 = _ffn(x1, n, sc2, sh2, w["g_pre_ffn"], w["w_ffn_in"], w["w_ffn_out"], w["g_post_ffn"], gt2)
    return x2.reshape(b, n, d)


def kernel(x_prompt, x_sample, c_prompt, c_sample, w_mod, b_mod, g_pre_mix, w_in, rpb, ret_decay_logit,
           ret_gn, w_up_na, w_up_ret, w_out, g_post_mix, g_pre_ffn, w_ffn_in, w_ffn_out, g_post_ffn):
    depth = w_mod.shape[0]
    nb = x_prompt.shape[0]
    c_all = jnp.concatenate([c_prompt, c_sample], axis=0)
    c_all = jnp.pad(c_all, ((0, -c_all.shape[0] % 8), (0, 0)))
    cos, sin = _rope_tables(max(x_prompt.shape[1], x_sample.shape[1]))
    y_prompt, y_sample = x_prompt, x_sample
    for l in range(depth):
        q_cols = lax.broadcasted_iota(jnp.int32, (1, IN_WIDTH), 1) < NA_HEADS * NA_HEAD_DIM
        col_scale = jnp.where(q_cols, NA_HEAD_DIM ** -0.5 * LOG2_E, 1.0).astype(F32)
        w = {"g_pre_mix": g_pre_mix[l].reshape(1, -1), "w_in": (w_in[l] * col_scale).astype(BF16),
             "w_up_na": w_up_na[l].astype(BF16), "w_up_ret": w_up_ret[l].astype(BF16),
             "w_out": w_out[l].astype(BF16), "g_post_mix": g_post_mix[l].reshape(1, -1),
             "g_pre_ffn": g_pre_ffn[l].reshape(1, -1), "w_ffn_in": w_ffn_in[l].astype(BF16),
             "w_ffn_out": w_ffn_out[l].astype(BF16), "g_post_ffn": g_post_ffn[l].reshape(1, -1)}
        mod = _modulation(c_all, w_mod[l], b_mod[l])
        bias_table = _na_bias_table(rpb[l])
        y_prompt = _encoder_layer(y_prompt, mod[:nb], cos, sin, bias_table, w, ret_decay_logit[l], ret_gn[l])
        y_sample = _encoder_layer(y_sample, mod[nb:nb + x_sample.shape[0]], cos, sin, bias_table, w, ret_decay_logit[l], ret_gn[l])
    return (y_prompt, y_sample)
```

```python
import functools

import jax
import jax.numpy as jnp
from jax import lax
from jax.experimental import pallas as pl
from jax.experimental.pallas import tpu as pltpu

D_MODEL = 1024
GRID_W = 64
NA_HEADS = 16
NA_HEAD_DIM = 64
NA_KH = 8
NA_KW = 16
RET_HEADS = 4
RET_KEY_DIM = 256
RET_VAL_DIM = 512
ROPE_BASE = 10000.0
FFN_HIDDEN = 2816
IN_WIDTH = 11264
N_MOD = 6
EPS = 1e-6

BF16 = jnp.bfloat16
F32 = jnp.float32

COL_NA_Q, COL_NA_K, COL_NA_V = 0, 1, 2
COL_RET_Q, COL_RET_K = 12, 16
COL_RET_V = 10
COL_RET_GATE, COL_G_NA, COL_G_RET = 7, 9, 10

INPROJ_TM = 512
INPROJ_CHUNK = 1024
RET_Q_START, RET_K_START, RET_V_START = 3072, 4096, 5120

NEG_BIAS = -1e30
NA_ROWS_PER_STEP = 8
NA_WIN_ROWS = NA_ROWS_PER_STEP + NA_KH
NA_PHASE_ROWS = 8
NA_PAIRS = NA_HEADS // 2
NA_GROUP_HEADS = 4
MXU_WIDTH = 256
LOG2_E = 1.4426950408889634
RET_CHUNK = 256
RET_BLOCK = 4096
MERGE_SLABS = 2
FFN_TM = 1024
FFN_SLABS = 4
VMEM_LIMIT = 56 * 1024 * 1024


def _params(semantics):
    return pltpu.CompilerParams(dimension_semantics=semantics, vmem_limit_bytes=VMEM_LIMIT)


def _silu(x):
    return x * jax.nn.sigmoid(x)


def _mod_kernel(c_ref, w_ref, b_ref, o_ref):
    s = _silu(c_ref[...]).astype(BF16)
    o_ref[...] = jnp.dot(s, w_ref[...].astype(BF16), preferred_element_type=F32) + b_ref[...]


def _modulation(c, w_mod, b_mod):
    rows = c.shape[0]
    width = w_mod.shape[1]
    tn = 1536
    return pl.pallas_call(
        _mod_kernel,
        grid=(width // tn,),
        in_specs=[pl.BlockSpec((rows, D_MODEL), lambda j: (0, 0)),
                  pl.BlockSpec((D_MODEL, tn), lambda j: (0, j)),
                  pl.BlockSpec((1, tn), lambda j: (0, j))],
        out_specs=pl.BlockSpec((rows, tn), lambda j: (0, j)),
        out_shape=jax.ShapeDtypeStruct((rows, width), F32),
        compiler_params=_params(("arbitrary",)),
        name="modulation",
    )(c, w_mod, b_mod.reshape(1, width))


def _rope_kernel(inv_ref, cos_ref, sin_ref):
    tb, half = cos_ref.shape
    nhi = tb // half
    inv = inv_ref[...]
    lo = lax.broadcasted_iota(jnp.int32, (half, half), 0).astype(F32)
    hi = ((pl.program_id(0) * nhi + lax.broadcasted_iota(jnp.int32, (nhi, half), 0)) * half).astype(F32)
    cos_lo, sin_lo = jnp.cos(lo * inv), jnp.sin(lo * inv)
    cos_hi, sin_hi = jnp.cos(hi * inv), jnp.sin(hi * inv)
    for r in range(nhi):
        rows = slice(r * half, (r + 1) * half)
        c1, s1 = cos_hi[r:r + 1, :], sin_hi[r:r + 1, :]
        cos_ref[rows, :] = c1 * cos_lo - s1 * sin_lo
        sin_ref[rows, :] = s1 * cos_lo + c1 * sin_lo


def _rope_tables(n):
    half = RET_KEY_DIM // 2
    inv = 1.0 / (ROPE_BASE ** (jnp.arange(half, dtype=F32) / half))
    tb = 2048
    spec = pl.BlockSpec((tb, half), lambda i: (i, 0))
    return pl.pallas_call(
        _rope_kernel,
        grid=(n // tb,),
        in_specs=[pl.BlockSpec((1, half), lambda i: (0, 0))],
        out_specs=[spec, spec],
        out_shape=[jax.ShapeDtypeStruct((n, half), F32)] * 2,
        compiler_params=_params(("parallel",)),
        name="rope_tables",
    )(inv.reshape(1, half))


def _modulated_rmsnorm(x, gain, scale, shift):
    y = x * lax.rsqrt(jnp.mean(x * x, axis=-1, keepdims=True) + EPS) * gain
    return y * (1.0 + scale) + shift


def _inproj_kernel(x_ref, sc_ref, sh_ref, g_ref, w_ref, cos_ref, sin_ref, o_ref, *, plain_chunk):
    h = _modulated_rmsnorm(x_ref[...], g_ref[...], sc_ref[0], sh_ref[0]).astype(BF16)
    half = RET_KEY_DIM // 2
    def plain(lo, hi):
        for c0 in range(lo, hi, plain_chunk):
            cols = slice(c0, min(c0 + plain_chunk, hi))
            o_ref[:, cols] = jnp.dot(h, w_ref[:, cols], preferred_element_type=F32).astype(o_ref.dtype)

    plain(0, RET_Q_START)
    for c0 in range(RET_Q_START, RET_V_START, INPROJ_CHUNK):
        scale = RET_KEY_DIM ** -0.5 if c0 >= RET_K_START else 1.0
        cos, sin = cos_ref[...] * scale, sin_ref[...] * scale
        for h0 in range(c0, c0 + INPROJ_CHUNK, RET_KEY_DIM):
            x = jnp.dot(h, w_ref[:, h0:h0 + RET_KEY_DIM], preferred_element_type=F32)
            x1, x2 = x[:, :half], x[:, half:]
            o_ref[:, h0:h0 + half] = (x1 * cos - x2 * sin).astype(o_ref.dtype)
            o_ref[:, h0 + half:h0 + RET_KEY_DIM] = (x1 * sin + x2 * cos).astype(o_ref.dtype)
    plain(RET_V_START, IN_WIDTH)


def _input_projection(x2d, n, scale, shift, gain, w_in, cos, sin, plain_chunk=INPROJ_CHUNK):
    t = x2d.shape[0]
    tm = INPROJ_TM
    per_seq = n // tm
    vec = pl.BlockSpec((1, 1, D_MODEL), lambda i: (i // per_seq, 0, 0))
    rope = pl.BlockSpec((tm, RET_KEY_DIM // 2), lambda i: (i % per_seq, 0))

    def const(shape):
        return pl.BlockSpec(shape, lambda i: (0, 0), pipeline_mode=pl.Buffered(1))

    return pl.pallas_call(
        functools.partial(_inproj_kernel, plain_chunk=plain_chunk),
        grid=(t // tm,),
        in_specs=[pl.BlockSpec((tm, D_MODEL), lambda i: (i, 0)),
                  vec, vec,
                  const((1, D_MODEL)), const((D_MODEL, IN_WIDTH)),
                  rope, rope],
        out_specs=pl.BlockSpec((tm, IN_WIDTH), lambda i: (i, 0)),
        out_shape=jax.ShapeDtypeStruct((t, IN_WIDTH), BF16),
        compiler_params=_params(("parallel",)),
        name="input_projection",
    )(x2d, scale, shift, gain, w_in, cos, sin)


def _bias_table_kernel(u_ref, o_ref):
    w = GRID_W
    n_rows = 2 * NA_KH - 1
    c = lax.broadcasted_iota(jnp.int32, (w, 2 * w), 0)
    lane = lax.broadcasted_iota(jnp.int32, (w, 2 * w), 1)
    key_col = lane & (w - 1)
    col_start = jnp.clip(c - NA_KW // 2, 0, w - NA_KW)
    in_window = (key_col >= col_start) & (key_col < col_start + NA_KW)
    first_row = lane < w
    for hl in range(2):
        rolled = []
        for s in range(n_rows):
            u = jnp.broadcast_to(u_ref[0, hl, s:s + 1, :], (w, 2 * w))
            rolled.append((pltpu.roll(u, 0, 1, stride=1, stride_axis=0),
                           pltpu.roll(u, w, 1, stride=1, stride_axis=0)))
        for s in range(n_rows - 1):
            tile = jnp.where(first_row, rolled[s][0], rolled[s + 1][1])
            o_ref[0, s, hl * w:(hl + 1) * w, :] = jnp.where(in_window, tile * LOG2_E, NEG_BIAS)


def _na_bias_table(rpb):
    w = GRID_W
    n_rows = 2 * NA_KH - 1
    rpb = rpb.astype(F32)
    gap = jnp.zeros(rpb.shape[:2] + (2 * w - (2 * NA_KW - 1),), F32)
    u = jnp.concatenate([rpb[..., NA_KW - 1:], gap, rpb[..., :NA_KW - 1]], axis=-1)
    u = u.reshape(NA_PAIRS, 2, n_rows, 2 * w)
    return pl.pallas_call(
        _bias_table_kernel,
        grid=(NA_PAIRS,),
        in_specs=[pl.BlockSpec((1, 2, n_rows, 2 * w), lambda p: (p, 0, 0, 0))],
        out_specs=pl.BlockSpec((1, n_rows - 1, 2 * w, 2 * w), lambda p: (p, 0, 0, 0)),
        out_shape=jax.ShapeDtypeStruct((NA_PAIRS, n_rows - 1, 2 * w, 2 * w), F32),
        compiler_params=_params(("parallel",)),
        name="na_bias_table",
    )(u)


def _na_window_start(i, rows_total):
    return jnp.clip(i * NA_ROWS_PER_STEP - NA_KH // 2, 0, rows_total - NA_WIN_ROWS)


def _na_kernel(q_ref, kw_ref, vw_ref, tb_ref, o_ref, s0_ref, s1_ref, p0_ref, p1_ref, *, rows_total):
    i = pl.program_id(1)
    s_refs, p_refs = (s0_ref, s1_ref), (p0_ref, p1_ref)
    w = GRID_W
    g = NA_GROUP_HEADS
    gw = g * w
    pw = 2 * w
    row_head = lax.broadcasted_iota(jnp.int32, (gw, gw), 0) // w
    lane_head = lax.broadcasted_iota(jnp.int32, (gw, gw), 1) // w
    own_head = jnp.where(row_head == lane_head, 1.0, 0.0).astype(BF16)
    first_head = lax.broadcasted_iota(jnp.int32, (w, pw), 1) < w
    win = NA_KH * w
    nj = NA_ROWS_PER_STEP
    ones = jnp.ones((win, MXU_WIDTH - pw), BF16)

    ws = _na_window_start(i, rows_total)
    offs, rels = [], []
    for j in range(nj):
        r = i * nj + j
        rs = jnp.clip(r - NA_KH // 2, 0, rows_total - NA_KH)
        offs.append(pl.multiple_of((rs - ws) * w, w))
        rels.append(rs - r + (NA_KH - 1))

    stage = [slice(j * gw, (j + 1) * gw) for j in range(nj)]
    phases = [(grp, range(j0, j0 + NA_PHASE_ROWS))
              for grp in range(NA_HEADS // g) for j0 in range(0, nj, NA_PHASE_ROWS)]
    for ph, (grp, js) in enumerate(phases):
        lanes = slice(grp * gw, (grp + 1) * gw)
        s_ref = s_refs[ph % 2]
        p_ref = p_refs[ph % 2]
        for j in js:
            qg = q_ref[0, j * w:(j + 1) * w, lanes]
            qs = jnp.concatenate([qg] * g, axis=0) * own_head
            kg = kw_ref[0, pl.ds(offs[j], win), lanes]
            s = lax.dot_general(qs, kg, (((1,), (1,)), ((), ())), preferred_element_type=F32)
            for pr in range(g // 2):
                for t in range(NA_KH // 2):
                    s_ref[j * gw + pr * 2 * w:j * gw + (pr + 1) * 2 * w, t * 2 * w:(t + 1) * 2 * w] = (
                        s[pr * 2 * w:(pr + 1) * 2 * w, t * 2 * w:(t + 1) * 2 * w]
                        + tb_ref[grp * (g // 2) + pr, rels[j] + 2 * t])
        for j in js:
            s = s_ref[stage[j], :]
            p_ref[stage[j], :] = jnp.exp2(s - jnp.max(s, axis=1, keepdims=True)).astype(BF16)
        for j in js:
            for pr in range(g // 2):
                pair_rows = slice(j * gw + pr * pw, j * gw + (pr + 1) * pw)
                pair_lanes = slice(grp * gw + pr * pw, grp * gw + (pr + 1) * pw)
                v_aug = jnp.concatenate([vw_ref[0, pl.ds(offs[j], win), pair_lanes], ones], axis=1)
                og = jnp.dot(p_ref[pair_rows, :], v_aug, preferred_element_type=F32)
                og = og[:, :pw] / og[:, pw:]
                o = jnp.where(first_head, og[:w], og[w:])
                o_ref[0, j * w:(j + 1) * w, pair_lanes] = o.astype(o_ref.dtype)


def _neighbourhood_attention(proj, bias_table):
    b, n, _ = proj.shape
    rows = n // GRID_W
    nblk = rows // NA_ROWS_PER_STEP
    blk = NA_ROWS_PER_STEP * GRID_W
    width = NA_HEADS * NA_HEAD_DIM
    stage_rows = NA_ROWS_PER_STEP * NA_GROUP_HEADS * GRID_W

    def window(col):
        return pl.BlockSpec((pl.Element(1), pl.Element(NA_WIN_ROWS * GRID_W), pl.Element(width)),
                            lambda bi, i: (bi, _na_window_start(i, rows) * GRID_W, col * width))

    return pl.pallas_call(
        functools.partial(_na_kernel, rows_total=rows),
        grid=(b, nblk),
        in_specs=[pl.BlockSpec((1, blk, width), lambda bi, i: (bi, i, COL_NA_Q)),
                  window(COL_NA_K), window(COL_NA_V),
                  pl.BlockSpec(bias_table.shape, lambda bi, i: (0, 0, 0, 0), pipeline_mode=pl.Buffered(1))],
        out_specs=pl.BlockSpec((1, blk, width), lambda bi, i: (bi, i, 0)),
        out_shape=jax.ShapeDtypeStruct((b, n, width), BF16),
        scratch_shapes=[pltpu.VMEM((stage_rows, NA_KH * GRID_W), F32), pltpu.VMEM((stage_rows, NA_KH * GRID_W), F32),
                        pltpu.VMEM((stage_rows, NA_KH * GRID_W), BF16), pltpu.VMEM((stage_rows, NA_KH * GRID_W), BF16)],
        compiler_params=_params(("parallel", "arbitrary")),
        name="neighbourhood_attention",
    )(proj, proj, proj, bias_table)


def _ret_direction(dl_ref, q_ref, k_ref, v_ref, o_ref, state_ref, fwd_ref, blk, backward):
    c = RET_CHUNK
    logit = dl_ref[1 if backward else 0, 0]
    lg = jnp.minimum(logit, 0.0) - jnp.log1p(jnp.exp(-jnp.abs(logit)))
    row = lax.broadcasted_iota(jnp.int32, (c, c), 0)
    col = lax.broadcasted_iota(jnp.int32, (c, c), 1)
    pos = lax.broadcasted_iota(jnp.int32, (c, 1), 0).astype(F32)
    if backward:
        rel = (col - row).astype(F32)
        mask = rel > 0
        xi = jnp.exp((c - pos) * lg)
        zeta = jnp.exp(pos * lg)
    else:
        rel = (row - col).astype(F32)
        mask = rel >= 0
        xi = jnp.exp((pos + 1.0) * lg)
        zeta = jnp.exp((c - 1.0 - pos) * lg)
    decay = jnp.where(mask, jnp.exp(jnp.where(mask, rel, 0.0) * lg), 0.0)
    chunk_decay = jnp.exp(c * lg)

    nsub = q_ref.shape[1] // c
    order = range(nsub - 1, -1, -1) if backward else range(nsub)
    for u in order:
        sl = slice(u * c, (u + 1) * c)
        qb = q_ref[0, sl, :]
        kb = k_ref[0, sl, :]
        v = v_ref[0, sl, :]
        scores = lax.dot_general(qb, kb, (((1,), (1,)), ((), ())), preferred_element_type=F32) * decay
        intra = jnp.dot(scores.astype(BF16), v, preferred_element_type=F32)
        state = state_ref[...]
        cross = jnp.dot(qb, state.astype(BF16), preferred_element_type=F32) * xi
        kz = (kb.astype(F32) * zeta).astype(BF16)
        update = lax.dot_general(kz, v, (((0,), (0,)), ((), ())), preferred_element_type=F32)
        state_ref[...] = state * chunk_decay + update
        y = intra + cross
        parked = pl.ds(pl.multiple_of(blk * RET_BLOCK + u * c, c), c)
        if backward:
            o_ref[0, sl, :] = (y + fwd_ref[parked, :].astype(F32)).astype(o_ref.dtype)
        else:
            fwd_ref[parked, :] = y.astype(fwd_ref.dtype)


def _ret_kernel(dl_ref, q_ref, k_ref, v_ref, o_ref, state_ref, fwd_ref):
    step = pl.program_id(2)
    nblk = pl.num_programs(2) // 2

    @pl.when(jnp.logical_or(step == 0, step == nblk))
    def _():
        state_ref[...] = jnp.zeros_like(state_ref)

    @pl.when(step < nblk)
    def _():
        _ret_direction(dl_ref, q_ref, k_ref, v_ref, o_ref, state_ref, fwd_ref, step, backward=False)

    @pl.when(step >= nblk)
    def _():
        _ret_direction(dl_ref, q_ref, k_ref, v_ref, o_ref, state_ref, fwd_ref, 2 * nblk - 1 - step,
                       backward=True)


def _retention(proj, decay_logit):
    b, n, _ = proj.shape
    nblk = n // RET_BLOCK

    def tok(step):
        return jnp.where(step < nblk, step, 2 * nblk - 1 - step)

    def out_tok(step):
        return jnp.where(step < nblk, nblk - 1, 2 * nblk - 1 - step)

    return pl.pallas_call(
        _ret_kernel,
        grid=(b, RET_HEADS, 2 * nblk),
        in_specs=[pl.BlockSpec((2, 1, 1, 1), lambda bi, h, st: (0, h, 0, 0)),
                  pl.BlockSpec((1, RET_BLOCK, RET_KEY_DIM), lambda bi, h, st: (bi, tok(st), COL_RET_Q + h)),
                  pl.BlockSpec((1, RET_BLOCK, RET_KEY_DIM), lambda bi, h, st: (bi, tok(st), COL_RET_K + h)),
                  pl.BlockSpec((1, RET_BLOCK, RET_VAL_DIM), lambda bi, h, st: (bi, tok(st), COL_RET_V + h))],
        out_specs=pl.BlockSpec((1, RET_BLOCK, RET_VAL_DIM), lambda bi, h, st: (bi, out_tok(st), h)),
        out_shape=jax.ShapeDtypeStruct((b, n, RET_HEADS * RET_VAL_DIM), BF16),
        scratch_shapes=[pltpu.VMEM((RET_KEY_DIM, RET_VAL_DIM), F32), pltpu.VMEM((n, RET_VAL_DIM), BF16)],
        compiler_params=_params(("parallel", "parallel", "arbitrary")),
        name="retention",
    )(decay_logit.reshape(2, RET_HEADS, 1, 1), proj, proj, proj)


def _merge_kernel(x_ref, yna_ref, yret_ref, rg0_ref, rg1_ref, gna_ref, gret_ref, gn_ref,
                  wna_ref, wret_ref, wout_ref, gpost_ref, gt_ref, o_ref):
    heads_per_ref = RET_HEADS // 2
    tm = x_ref.shape[0]
    for slab in range(MERGE_SLABS):
        rows = slice(slab * tm // MERGE_SLABS, (slab + 1) * tm // MERGE_SLABS)
        a = jnp.dot(yna_ref[rows, :], wna_ref[...], preferred_element_type=F32)
        gated = []
        for hd in range(RET_HEADS):
            cols = slice(hd * RET_VAL_DIM, (hd + 1) * RET_VAL_DIM)
            y = yret_ref[rows, cols].astype(F32)
            mu = jnp.mean(y, axis=-1, keepdims=True)
            yc = y - mu
            var = jnp.mean(yc * yc, axis=-1, keepdims=True)
            yn = yc * lax.rsqrt(var + EPS) * gn_ref[:, cols]
            rg_ref = rg0_ref if hd < heads_per_ref else rg1_ref
            gcols = slice((hd % heads_per_ref) * RET_VAL_DIM, (hd % heads_per_ref + 1) * RET_VAL_DIM)
            gated.append((_silu(rg_ref[rows, gcols].astype(F32)) * yn).astype(BF16))
        r = jnp.dot(jnp.concatenate(gated, axis=1), wret_ref[...], preferred_element_type=F32)
        merged = (jax.nn.sigmoid(gna_ref[rows, :].astype(F32)) * a
                  + jax.nn.sigmoid(gret_ref[rows, :].astype(F32)) * r)
        out = jnp.dot(merged.astype(BF16), wout_ref[...], preferred_element_type=F32)
        y = out * lax.rsqrt(jnp.mean(out * out, axis=-1, keepdims=True) + EPS) * gpost_ref[...]
        o_ref[rows, :] = x_ref[rows, :] + gt_ref[0] * y


def _merge(x2d, n, y_na, y_ret, proj2d, ret_gn, w_up_na, w_up_ret, w_out, g_post, gate):
    t = x2d.shape[0]
    tm = 512
    per_seq = n // tm
    ret_w = RET_HEADS * RET_VAL_DIM

    def const(shape):
        return pl.BlockSpec(shape, lambda i: (0, 0), pipeline_mode=pl.Buffered(1))

    def proj_cols(col):
        return pl.BlockSpec((tm, D_MODEL), lambda i: (i, col))

    return pl.pallas_call(
        _merge_kernel,
        grid=(t // tm,),
        in_specs=[pl.BlockSpec((tm, D_MODEL), lambda i: (i, 0)),
                  pl.BlockSpec((tm, D_MODEL), lambda i: (i, 0)),
                  pl.BlockSpec((tm, ret_w), lambda i: (i, 0)),
                  proj_cols(COL_RET_GATE), proj_cols(COL_RET_GATE + 1),
                  proj_cols(COL_G_NA), proj_cols(COL_G_RET),
                  const((1, ret_w)),
                  const((D_MODEL, D_MODEL)), const((ret_w, D_MODEL)), const((D_MODEL, D_MODEL)),
                  const((1, D_MODEL)),
                  pl.BlockSpec((1, 1, D_MODEL), lambda i: (i // per_seq, 0, 0))],
        out_specs=pl.BlockSpec((tm, D_MODEL), lambda i: (i, 0)),
        out_shape=jax.ShapeDtypeStruct((t, D_MODEL), F32),
        compiler_params=_params(("parallel",)),
        name="merge_out_projection",
    )(x2d, y_na, y_ret, proj2d, proj2d, proj2d, proj2d, ret_gn.reshape(1, ret_w),
      w_up_na, w_up_ret, w_out, g_post, gate)


def _ffn_kernel(x_ref, sc_ref, sh_ref, gpre_ref, win_ref, wout_ref, gpost_ref, gt_ref, o_ref):
    tm = x_ref.shape[0]
    for slab in range(FFN_SLABS):
        rows = slice(slab * tm // FFN_SLABS, (slab + 1) * tm // FFN_SLABS)
        x = x_ref[rows, :]
        h = _modulated_rmsnorm(x, gpre_ref[...], sc_ref[0], sh_ref[0]).astype(BF16)
        a = jnp.dot(h, win_ref[:, :FFN_HIDDEN], preferred_element_type=F32)
        g = jnp.dot(h, win_ref[:, FFN_HIDDEN:], preferred_element_type=F32)
        f = jnp.dot((_silu(a) * g).astype(BF16), wout_ref[...], preferred_element_type=F32)
        y = f * lax.rsqrt(jnp.mean(f * f, axis=-1, keepdims=True) + EPS) * gpost_ref[...]
        o_ref[rows, :] = x + gt_ref[0] * y


def _ffn(x2d, n, scale, shift, g_pre, w_ffn_in, w_ffn_out, g_post, gate):
    t = x2d.shape[0]
    tm = FFN_TM
    per_seq = n // tm
    vec = pl.BlockSpec((1, 1, D_MODEL), lambda i: (i // per_seq, 0, 0))

    def const(shape):
        return pl.BlockSpec(shape, lambda i: (0, 0), pipeline_mode=pl.Buffered(1))

    return pl.pallas_call(
        _ffn_kernel,
        grid=(t // tm,),
        in_specs=[pl.BlockSpec((tm, D_MODEL), lambda i: (i, 0)),
                  vec, vec, const((1, D_MODEL)),
                  const((D_MODEL, 2 * FFN_HIDDEN)), const((FFN_HIDDEN, D_MODEL)),
                  const((1, D_MODEL)), vec],
        out_specs=pl.BlockSpec((tm, D_MODEL), lambda i: (i, 0)),
        out_shape=jax.ShapeDtypeStruct((t, D_MODEL), F32),
        compiler_params=_params(("parallel",)),
        name="swiglu_ffn",
    )(x2d, scale, shift, g_pre, w_ffn_in, w_ffn_out, g_post, gate)


def _encoder_layer(x, mod, cos, sin, bias_table, w, decay_logit, ret_gn):
    b, n, d = x.shape
    sh1, sc1, gt1, sh2, sc2, gt2 = [m.reshape(b, 1, d) for m in jnp.split(mod, N_MOD, axis=-1)]
    x2d = x.reshape(b * n, d)
    proj2d = _input_projection(x2d, n, sc1, sh1, w["g_pre_mix"], w["w_in"], cos, sin,
                               plain_chunk=2048 if b == 1 else INPROJ_CHUNK)
    proj = proj2d.reshape(b, n, IN_WIDTH)
    y_na = _neighbourhood_attention(proj, bias_table)
    y_ret = _retention(proj, decay_logit)
    x1 = _merge(x2d, n, y_na.reshape(b * n, -1), y_ret.reshape(b * n, -1), proj2d, ret_gn,
                w["w_up_na"], w["w_up_ret"], w["w_out"], w["g_post_mix"], gt1)
    x2 = _ffn(x1, n, sc2, sh2, w["g_pre_ffn"], w["w_ffn_in"], w["w_ffn_out"], w["g_post_ffn"], gt2)
    return x2.reshape(b, n, d)


def kernel(x_prompt, x_sample, c_prompt, c_sample, w_mod, b_mod, g_pre_mix, w_in, rpb, ret_decay_logit,
           ret_gn, w_up_na, w_up_ret, w_out, g_post_mix, g_pre_ffn, w_ffn_in, w_ffn_out, g_post_ffn):
    depth = w_mod.shape[0]
    nb = x_prompt.shape[0]
    c_all = jnp.concatenate([c_prompt, c_sample], axis=0)
    c_all = jnp.pad(c_all, ((0, -c_all.shape[0] % 8), (0, 0)))
    cos, sin = _rope_tables(max(x_prompt.shape[1], x_sample.shape[1]))
    y_prompt, y_sample = x_prompt, x_sample
    for l in range(depth):
        q_cols = lax.broadcasted_iota(jnp.int32, (1, IN_WIDTH), 1) < NA_HEADS * NA_HEAD_DIM
        col_scale = jnp.where(q_cols, NA_HEAD_DIM ** -0.5 * LOG2_E, 1.0).astype(F32)
        w = {"g_pre_mix": g_pre_mix[l].reshape(1, -1), "w_in": (w_in[l] * col_scale).astype(BF16),
             "w_up_na": w_up_na[l].astype(BF16), "w_up_ret": w_up_ret[l].astype(BF16),
             "w_out": w_out[l].astype(BF16), "g_post_mix": g_post_mix[l].reshape(1, -1),
             "g_pre_ffn": g_pre_ffn[l].reshape(1, -1), "w_ffn_in": w_ffn_in[l].astype(BF16),
             "w_ffn_out": w_ffn_out[l].astype(BF16), "g_post_ffn": g_post_ffn[l].reshape(1, -1)}
        mod = _modulation(c_all, w_mod[l], b_mod[l])
        bias_table = _na_bias_table(rpb[l])
        y_prompt = _encoder_layer(y_prompt, mod[:nb], cos, sin, bias_table, w, ret_decay_logit[l], ret_gn[l])
        y_sample = _encoder_layer(y_sample, mod[nb:nb + x_sample.shape[0]], cos, sin, bias_table, w, ret_decay_logit[l], ret_gn[l])
    return (y_prompt, y_sample)
```

```python
import functools

import jax
import jax.numpy as jnp
from jax import lax
from jax.experimental import pallas as pl
from jax.experimental.pallas import tpu as pltpu

D_MODEL = 1024
GRID_W = 64
NA_HEADS = 16
NA_HEAD_DIM = 64
NA_KH = 8
NA_KW = 16
RET_HEADS = 4
RET_KEY_DIM = 256
RET_VAL_DIM = 512
ROPE_BASE = 10000.0
FFN_HIDDEN = 2816
IN_WIDTH = 11264
N_MOD = 6
EPS = 1e-6

BF16 = jnp.bfloat16
F32 = jnp.float32

COL_NA_Q, COL_NA_K, COL_NA_V = 0, 1, 2
COL_RET_Q, COL_RET_K = 12, 16
COL_RET_V = 10
COL_RET_GATE, COL_G_NA, COL_G_RET = 7, 9, 10

INPROJ_TM = 512
INPROJ_CHUNK = 1024
RET_Q_START, RET_K_START, RET_V_START = 3072, 4096, 5120

NEG_BIAS = -1e30
NA_ROWS_PER_STEP = 8
NA_WIN_ROWS = NA_ROWS_PER_STEP + NA_KH
NA_PHASE_ROWS = 8
NA_PAIRS = NA_HEADS // 2
NA_GROUP_HEADS = 4
MXU_WIDTH = 256
LOG2_E = 1.4426950408889634
RET_CHUNK = 256
RET_BLOCK_2WAY = 2048
MERGE_SLABS = 2
FFN_TM = 1024
FFN_SLABS = 4
VMEM_LIMIT = 56 * 1024 * 1024


def _params(semantics):
    return pltpu.CompilerParams(dimension_semantics=semantics, vmem_limit_bytes=VMEM_LIMIT)


def _silu(x):
    return x * jax.nn.sigmoid(x)


def _mod_kernel(c_ref, w_ref, b_ref, o_ref):
    s = _silu(c_ref[...]).astype(BF16)
    o_ref[...] = jnp.dot(s, w_ref[...].astype(BF16), preferred_element_type=F32) + b_ref[...]


def _modulation(c, w_mod, b_mod):
    rows = c.shape[0]
    width = w_mod.shape[1]
    tn = 1536
    return pl.pallas_call(
        _mod_kernel,
        grid=(width // tn,),
        in_specs=[pl.BlockSpec((rows, D_MODEL), lambda j: (0, 0)),
                  pl.BlockSpec((D_MODEL, tn), lambda j: (0, j)),
                  pl.BlockSpec((1, tn), lambda j: (0, j))],
        out_specs=pl.BlockSpec((rows, tn), lambda j: (0, j)),
        out_shape=jax.ShapeDtypeStruct((rows, width), F32),
        compiler_params=_params(("arbitrary",)),
        name="modulation",
    )(c, w_mod, b_mod.reshape(1, width))


def _rope_kernel(inv_ref, cos_ref, sin_ref):
    tb, half = cos_ref.shape
    nhi = tb // half
    inv = inv_ref[...]
    lo = lax.broadcasted_iota(jnp.int32, (half, half), 0).astype(F32)
    hi = ((pl.program_id(0) * nhi + lax.broadcasted_iota(jnp.int32, (nhi, half), 0)) * half).astype(F32)
    cos_lo, sin_lo = jnp.cos(lo * inv), jnp.sin(lo * inv)
    cos_hi, sin_hi = jnp.cos(hi * inv), jnp.sin(hi * inv)
    for r in range(nhi):
        rows = slice(r * half, (r + 1) * half)
        c1, s1 = cos_hi[r:r + 1, :], sin_hi[r:r + 1, :]
        cos_ref[rows, :] = c1 * cos_lo - s1 * sin_lo
        sin_ref[rows, :] = s1 * cos_lo + c1 * sin_lo


def _rope_tables(n):
    half = RET_KEY_DIM // 2
    inv = 1.0 / (ROPE_BASE ** (jnp.arange(half, dtype=F32) / half))
    tb = 2048
    spec = pl.BlockSpec((tb, half), lambda i: (i, 0))
    return pl.pallas_call(
        _rope_kernel,
        grid=(n // tb,),
        in_specs=[pl.BlockSpec((1, half), lambda i: (0, 0))],
        out_specs=[spec, spec],
        out_shape=[jax.ShapeDtypeStruct((n, half), F32)] * 2,
        compiler_params=_params(("parallel",)),
        name="rope_tables",
    )(inv.reshape(1, half))


def _modulated_rmsnorm(x, gain, scale, shift):
    y = x * lax.rsqrt(jnp.mean(x * x, axis=-1, keepdims=True) + EPS) * gain
    return y * (1.0 + scale) + shift


def _inproj_kernel(x_ref, sc_ref, sh_ref, g_ref, w_ref, cos_ref, sin_ref, o_ref):
    h = _modulated_rmsnorm(x_ref[...], g_ref[...], sc_ref[0], sh_ref[0]).astype(BF16)
    half = RET_KEY_DIM // 2
    for c0 in range(0, IN_WIDTH, INPROJ_CHUNK):
        if RET_Q_START <= c0 < RET_V_START:
            scale = RET_KEY_DIM ** -0.5 if c0 >= RET_K_START else 1.0
            cos, sin = cos_ref[...] * scale, sin_ref[...] * scale
            for h0 in range(c0, c0 + INPROJ_CHUNK, RET_KEY_DIM):
                x = jnp.dot(h, w_ref[:, h0:h0 + RET_KEY_DIM], preferred_element_type=F32)
                x1, x2 = x[:, :half], x[:, half:]
                o_ref[:, h0:h0 + half] = (x1 * cos - x2 * sin).astype(o_ref.dtype)
                o_ref[:, h0 + half:h0 + RET_KEY_DIM] = (x1 * sin + x2 * cos).astype(o_ref.dtype)
        else:
            cols = slice(c0, c0 + INPROJ_CHUNK)
            o_ref[:, cols] = jnp.dot(h, w_ref[:, cols], preferred_element_type=F32).astype(o_ref.dtype)


def _input_projection(x2d, n, scale, shift, gain, w_in, cos, sin):
    t = x2d.shape[0]
    tm = INPROJ_TM
    per_seq = n // tm
    vec = pl.BlockSpec((1, 1, D_MODEL), lambda i: (i // per_seq, 0, 0))
    rope = pl.BlockSpec((tm, RET_KEY_DIM // 2), lambda i: (i % per_seq, 0))

    def const(shape):
        return pl.BlockSpec(shape, lambda i: (0, 0), pipeline_mode=pl.Buffered(1))

    return pl.pallas_call(
        _inproj_kernel,
        grid=(t // tm,),
        in_specs=[pl.BlockSpec((tm, D_MODEL), lambda i: (i, 0)),
                  vec, vec,
                  const((1, D_MODEL)), const((D_MODEL, IN_WIDTH)),
                  rope, rope],
        out_specs=pl.BlockSpec((tm, IN_WIDTH), lambda i: (i, 0)),
        out_shape=jax.ShapeDtypeStruct((t, IN_WIDTH), BF16),
        compiler_params=_params(("parallel",)),
        name="input_projection",
    )(x2d, scale, shift, gain, w_in, cos, sin)


def _bias_table_kernel(u_ref, o_ref):
    w = GRID_W
    n_rows = 2 * NA_KH - 1
    c = lax.broadcasted_iota(jnp.int32, (w, 2 * w), 0)
    lane = lax.broadcasted_iota(jnp.int32, (w, 2 * w), 1)
    key_col = lane & (w - 1)
    col_start = jnp.clip(c - NA_KW // 2, 0, w - NA_KW)
    in_window = (key_col >= col_start) & (key_col < col_start + NA_KW)
    first_row = lane < w
    for hl in range(2):
        rolled = []
        for s in range(n_rows):
            u = jnp.broadcast_to(u_ref[0, hl, s:s + 1, :], (w, 2 * w))
            rolled.append((pltpu.roll(u, 0, 1, stride=1, stride_axis=0),
                           pltpu.roll(u, w, 1, stride=1, stride_axis=0)))
        for s in range(n_rows - 1):
            tile = jnp.where(first_row, rolled[s][0], rolled[s + 1][1])
            o_ref[0, s, hl * w:(hl + 1) * w, :] = jnp.where(in_window, tile * LOG2_E, NEG_BIAS)


def _na_bias_table(rpb):
    w = GRID_W
    n_rows = 2 * NA_KH - 1
    rpb = rpb.astype(F32)
    gap = jnp.zeros(rpb.shape[:2] + (2 * w - (2 * NA_KW - 1),), F32)
    u = jnp.concatenate([rpb[..., NA_KW - 1:], gap, rpb[..., :NA_KW - 1]], axis=-1)
    u = u.reshape(NA_PAIRS, 2, n_rows, 2 * w)
    return pl.pallas_call(
        _bias_table_kernel,
        grid=(NA_PAIRS,),
        in_specs=[pl.BlockSpec((1, 2, n_rows, 2 * w), lambda p: (p, 0, 0, 0))],
        out_specs=pl.BlockSpec((1, n_rows - 1, 2 * w, 2 * w), lambda p: (p, 0, 0, 0)),
        out_shape=jax.ShapeDtypeStruct((NA_PAIRS, n_rows - 1, 2 * w, 2 * w), F32),
        compiler_params=_params(("parallel",)),
        name="na_bias_table",
    )(u)


def _na_window_start(i, rows_total):
    return jnp.clip(i * NA_ROWS_PER_STEP - NA_KH // 2, 0, rows_total - NA_WIN_ROWS)


def _na_kernel(q_ref, kw_ref, vw_ref, tb_ref, o_ref, s0_ref, s1_ref, p0_ref, p1_ref, *, rows_total):
    i = pl.program_id(1)
    s_refs, p_refs = (s0_ref, s1_ref), (p0_ref, p1_ref)
    w = GRID_W
    g = NA_GROUP_HEADS
    gw = g * w
    pw = 2 * w
    row_head = lax.broadcasted_iota(jnp.int32, (gw, gw), 0) // w
    lane_head = lax.broadcasted_iota(jnp.int32, (gw, gw), 1) // w
    own_head = jnp.where(row_head == lane_head, 1.0, 0.0).astype(BF16)
    first_head = lax.broadcasted_iota(jnp.int32, (w, pw), 1) < w
    win = NA_KH * w
    nj = NA_ROWS_PER_STEP
    ones = jnp.ones((win, MXU_WIDTH - pw), BF16)

    ws = _na_window_start(i, rows_total)
    offs, rels = [], []
    for j in range(nj):
        r = i * nj + j
        rs = jnp.clip(r - NA_KH // 2, 0, rows_total - NA_KH)
        offs.append(pl.multiple_of((rs - ws) * w, w))
        rels.append(rs - r + (NA_KH - 1))

    stage = [slice(j * gw, (j + 1) * gw) for j in range(nj)]
    phases = [(grp, range(j0, j0 + NA_PHASE_ROWS))
              for grp in range(NA_HEADS // g) for j0 in range(0, nj, NA_PHASE_ROWS)]
    for ph, (grp, js) in enumerate(phases):
        lanes = slice(grp * gw, (grp + 1) * gw)
        s_ref = s_refs[ph % 2]
        p_ref = p_refs[ph % 2]
        for j in js:
            qg = q_ref[0, j * w:(j + 1) * w, lanes]
            qs = jnp.concatenate([qg] * g, axis=0) * own_head
            kg = kw_ref[0, pl.ds(offs[j], win), lanes]
            s = lax.dot_general(qs, kg, (((1,), (1,)), ((), ())), preferred_element_type=F32)
            for pr in range(g // 2):
                for t in range(NA_KH // 2):
                    s_ref[j * gw + pr * 2 * w:j * gw + (pr + 1) * 2 * w, t * 2 * w:(t + 1) * 2 * w] = (
                        s[pr * 2 * w:(pr + 1) * 2 * w, t * 2 * w:(t + 1) * 2 * w]
                        + tb_ref[grp * (g // 2) + pr, rels[j] + 2 * t])
        for j in js:
            s = s_ref[stage[j], :]
            p_ref[stage[j], :] = jnp.exp2(s - jnp.max(s, axis=1, keepdims=True)).astype(BF16)
        for j in js:
            for pr in range(g // 2):
                pair_rows = slice(j * gw + pr * pw, j * gw + (pr + 1) * pw)
                pair_lanes = slice(grp * gw + pr * pw, grp * gw + (pr + 1) * pw)
                v_aug = jnp.concatenate([vw_ref[0, pl.ds(offs[j], win), pair_lanes], ones], axis=1)
                og = jnp.dot(p_ref[pair_rows, :], v_aug, preferred_element_type=F32)
                og = og[:, :pw] / og[:, pw:]
                o = jnp.where(first_head, og[:w], og[w:])
                o_ref[0, j * w:(j + 1) * w, pair_lanes] = o.astype(o_ref.dtype)


def _neighbourhood_attention(proj, bias_table):
    b, n, _ = proj.shape
    rows = n // GRID_W
    nblk = rows // NA_ROWS_PER_STEP
    blk = NA_ROWS_PER_STEP * GRID_W
    width = NA_HEADS * NA_HEAD_DIM
    stage_rows = NA_ROWS_PER_STEP * NA_GROUP_HEADS * GRID_W

    def window(col):
        return pl.BlockSpec((pl.Element(1), pl.Element(NA_WIN_ROWS * GRID_W), pl.Element(width)),
                            lambda bi, i: (bi, _na_window_start(i, rows) * GRID_W, col * width))

    return pl.pallas_call(
        functools.partial(_na_kernel, rows_total=rows),
        grid=(b, nblk),
        in_specs=[pl.BlockSpec((1, blk, width), lambda bi, i: (bi, i, COL_NA_Q)),
                  window(COL_NA_K), window(COL_NA_V),
                  pl.BlockSpec(bias_table.shape, lambda bi, i: (0, 0, 0, 0), pipeline_mode=pl.Buffered(1))],
        out_specs=pl.BlockSpec((1, blk, width), lambda bi, i: (bi, i, 0)),
        out_shape=jax.ShapeDtypeStruct((b, n, width), BF16),
        scratch_shapes=[pltpu.VMEM((stage_rows, NA_KH * GRID_W), F32), pltpu.VMEM((stage_rows, NA_KH * GRID_W), F32),
                        pltpu.VMEM((stage_rows, NA_KH * GRID_W), BF16), pltpu.VMEM((stage_rows, NA_KH * GRID_W), BF16)],
        compiler_params=_params(("parallel", "arbitrary")),
        name="neighbourhood_attention",
    )(proj, proj, proj, bias_table)


def _ret_consts(dl_ref, backward):
    c = RET_CHUNK
    logit = dl_ref[1 if backward else 0, 0]
    lg = jnp.minimum(logit, 0.0) - jnp.log1p(jnp.exp(-jnp.abs(logit)))
    row = lax.broadcasted_iota(jnp.int32, (c, c), 0)
    col = lax.broadcasted_iota(jnp.int32, (c, c), 1)
    pos = lax.broadcasted_iota(jnp.int32, (c, 1), 0).astype(F32)
    if backward:
        rel = (col - row).astype(F32)
        mask = rel > 0
        xi = jnp.exp((c - pos) * lg)
        zeta = jnp.exp(pos * lg)
    else:
        rel = (row - col).astype(F32)
        mask = rel >= 0
        xi = jnp.exp((pos + 1.0) * lg)
        zeta = jnp.exp((c - 1.0 - pos) * lg)
    decay = jnp.where(mask, jnp.exp(jnp.where(mask, rel, 0.0) * lg), 0.0)
    return decay, xi, zeta, jnp.exp(c * lg)


def _ret_chunk(q_ref, k_ref, v_ref, o_ref, state_ref, consts, u):
    decay, xi, zeta, chunk_decay = consts
    sl = slice(u * RET_CHUNK, (u + 1) * RET_CHUNK)
    qb = q_ref[0, sl, :]
    kb = k_ref[0, sl, :]
    v = v_ref[0, sl, :]
    scores = lax.dot_general(qb, kb, (((1,), (1,)), ((), ())), preferred_element_type=F32) * decay
    intra = jnp.dot(scores.astype(BF16), v, preferred_element_type=F32)
    state = state_ref[...]
    cross = jnp.dot(qb, state.astype(BF16), preferred_element_type=F32) * xi
    kz = (kb.astype(F32) * zeta).astype(BF16)
    update = lax.dot_general(kz, v, (((0,), (0,)), ((), ())), preferred_element_type=F32)
    state_ref[...] = state * chunk_decay + update
    o_ref[0, sl, :] = (intra + cross).astype(o_ref.dtype)


def _ret_kernel(dl_ref, qf_ref, kf_ref, vf_ref, qb_ref, kb_ref, vb_ref, of_ref, ob_ref, sf_ref, sb_ref):
    @pl.when(pl.program_id(2) == 0)
    def _():
        sf_ref[...] = jnp.zeros_like(sf_ref)
        sb_ref[...] = jnp.zeros_like(sb_ref)

    fwd_consts = _ret_consts(dl_ref, backward=False)
    bwd_consts = _ret_consts(dl_ref, backward=True)
    nsub = qf_ref.shape[1] // RET_CHUNK
    for u in range(nsub):
        _ret_chunk(qf_ref, kf_ref, vf_ref, of_ref, sf_ref, fwd_consts, u)
        _ret_chunk(qb_ref, kb_ref, vb_ref, ob_ref, sb_ref, bwd_consts, nsub - 1 - u)


def _retention(proj, decay_logit):
    b, n, _ = proj.shape
    blk = RET_BLOCK_2WAY
    nblk = n // blk

    def specs(tok):
        return [pl.BlockSpec((1, blk, RET_KEY_DIM), lambda bi, h, st: (bi, tok(st), COL_RET_Q + h)),
                pl.BlockSpec((1, blk, RET_KEY_DIM), lambda bi, h, st: (bi, tok(st), COL_RET_K + h)),
                pl.BlockSpec((1, blk, RET_VAL_DIM), lambda bi, h, st: (bi, tok(st), COL_RET_V + h))]

    fwd_tok = lambda st: st
    bwd_tok = lambda st: nblk - 1 - st
    out_shape = jax.ShapeDtypeStruct((b, n, RET_HEADS * RET_VAL_DIM), BF16)
    return pl.pallas_call(
        _ret_kernel,
        grid=(b, RET_HEADS, nblk),
        in_specs=[pl.BlockSpec((2, 1, 1, 1), lambda bi, h, st: (0, h, 0, 0))] + specs(fwd_tok) + specs(bwd_tok),
        out_specs=[pl.BlockSpec((1, blk, RET_VAL_DIM), lambda bi, h, st: (bi, fwd_tok(st), h)),
                   pl.BlockSpec((1, blk, RET_VAL_DIM), lambda bi, h, st: (bi, bwd_tok(st), h))],
        out_shape=[out_shape, out_shape],
        scratch_shapes=[pltpu.VMEM((RET_KEY_DIM, RET_VAL_DIM), F32), pltpu.VMEM((RET_KEY_DIM, RET_VAL_DIM), F32)],
        compiler_params=_params(("parallel", "parallel", "arbitrary")),
        name="retention",
    )(decay_logit.reshape(2, RET_HEADS, 1, 1), proj, proj, proj, proj, proj, proj)


def _merge_kernel(x_ref, yna_ref, yret_ref, yret_b_ref, rg0_ref, rg1_ref, gna_ref, gret_ref, gn_ref,
                  wna_ref, wret_ref, wout_ref, gpost_ref, gt_ref, o_ref):
    heads_per_ref = RET_HEADS // 2
    tm = x_ref.shape[0]
    for slab in range(MERGE_SLABS):
        rows = slice(slab * tm // MERGE_SLABS, (slab + 1) * tm // MERGE_SLABS)
        a = jnp.dot(yna_ref[rows, :], wna_ref[...], preferred_element_type=F32)
        gated = []
        for hd in range(RET_HEADS):
            cols = slice(hd * RET_VAL_DIM, (hd + 1) * RET_VAL_DIM)
            y = yret_ref[rows, cols].astype(F32) + yret_b_ref[rows, cols].astype(F32)
            mu = jnp.mean(y, axis=-1, keepdims=True)
            yc = y - mu
            var = jnp.mean(yc * yc, axis=-1, keepdims=True)
            yn = yc * lax.rsqrt(var + EPS) * gn_ref[:, cols]
            rg_ref = rg0_ref if hd < heads_per_ref else rg1_ref
            gcols = slice((hd % heads_per_ref) * RET_VAL_DIM, (hd % heads_per_ref + 1) * RET_VAL_DIM)
            gated.append((_silu(rg_ref[rows, gcols].astype(F32)) * yn).astype(BF16))
        r = jnp.dot(jnp.concatenate(gated, axis=1), wret_ref[...], preferred_element_type=F32)
        merged = (jax.nn.sigmoid(gna_ref[rows, :].astype(F32)) * a
                  + jax.nn.sigmoid(gret_ref[rows, :].astype(F32)) * r)
        out = jnp.dot(merged.astype(BF16), wout_ref[...], preferred_element_type=F32)
        y = out * lax.rsqrt(jnp.mean(out * out, axis=-1, keepdims=True) + EPS) * gpost_ref[...]
        o_ref[rows, :] = x_ref[rows, :] + gt_ref[0] * y


def _merge(x2d, n, y_na, y_ret, y_ret_b, proj2d, ret_gn, w_up_na, w_up_ret, w_out, g_post, gate):
    t = x2d.shape[0]
    tm = 512
    per_seq = n // tm
    ret_w = RET_HEADS * RET_VAL_DIM

    def const(shape):
        return pl.BlockSpec(shape, lambda i: (0, 0), pipeline_mode=pl.Buffered(1))

    def proj_cols(col):
        return pl.BlockSpec((tm, D_MODEL), lambda i: (i, col))

    return pl.pallas_call(
        _merge_kernel,
        grid=(t // tm,),
        in_specs=[pl.BlockSpec((tm, D_MODEL), lambda i: (i, 0)),
                  pl.BlockSpec((tm, D_MODEL), lambda i: (i, 0)),
                  pl.BlockSpec((tm, ret_w), lambda i: (i, 0)),
                  pl.BlockSpec((tm, ret_w), lambda i: (i, 0)),
                  proj_cols(COL_RET_GATE), proj_cols(COL_RET_GATE + 1),
                  proj_cols(COL_G_NA), proj_cols(COL_G_RET),
                  const((1, ret_w)),
                  const((D_MODEL, D_MODEL)), const((ret_w, D_MODEL)), const((D_MODEL, D_MODEL)),
                  const((1, D_MODEL)),
                  pl.BlockSpec((1, 1, D_MODEL), lambda i: (i // per_seq, 0, 0))],
        out_specs=pl.BlockSpec((tm, D_MODEL), lambda i: (i, 0)),
        out_shape=jax.ShapeDtypeStruct((t, D_MODEL), F32),
        compiler_params=_params(("parallel",)),
        name="merge_out_projection",
    )(x2d, y_na, y_ret, y_ret_b, proj2d, proj2d, proj2d, proj2d, ret_gn.reshape(1, ret_w),
      w_up_na, w_up_ret, w_out, g_post, gate)


def _ffn_kernel(x_ref, sc_ref, sh_ref, gpre_ref, win_ref, wout_ref, gpost_ref, gt_ref, o_ref):
    tm = x_ref.shape[0]
    for slab in range(FFN_SLABS):
        rows = slice(slab * tm // FFN_SLABS, (slab + 1) * tm // FFN_SLABS)
        x = x_ref[rows, :]
        h = _modulated_rmsnorm(x, gpre_ref[...], sc_ref[0], sh_ref[0]).astype(BF16)
        a = jnp.dot(h, win_ref[:, :FFN_HIDDEN], preferred_element_type=F32)
        g = jnp.dot(h, win_ref[:, FFN_HIDDEN:], preferred_element_type=F32)
        f = jnp.dot((_silu(a) * g).astype(BF16), wout_ref[...], preferred_element_type=F32)
        y = f * lax.rsqrt(jnp.mean(f * f, axis=-1, keepdims=True) + EPS) * gpost_ref[...]
        o_ref[rows, :] = x + gt_ref[0] * y


def _ffn(x2d, n, scale, shift, g_pre, w_ffn_in, w_ffn_out, g_post, gate):
    t = x2d.shape[0]
    tm = FFN_TM
    per_seq = n // tm
    vec = pl.BlockSpec((1, 1, D_MODEL), lambda i: (i // per_seq, 0, 0))

    def const(shape):
        return pl.BlockSpec(shape, lambda i: (0, 0), pipeline_mode=pl.Buffered(1))

    return pl.pallas_call(
        _ffn_kernel,
        grid=(t // tm,),
        in_specs=[pl.BlockSpec((tm, D_MODEL), lambda i: (i, 0)),
                  vec, vec, const((1, D_MODEL)),
                  const((D_MODEL, 2 * FFN_HIDDEN)), const((FFN_HIDDEN, D_MODEL)),
                  const((1, D_MODEL)), vec],
        out_specs=pl.BlockSpec((tm, D_MODEL), lambda i: (i, 0)),
        out_shape=jax.ShapeDtypeStruct((t, D_MODEL), F32),
        compiler_params=_params(("parallel",)),
        name="swiglu_ffn",
    )(x2d, scale, shift, g_pre, w_ffn_in, w_ffn_out, g_post, gate)


def _encoder_layer(x, mod, cos, sin, bias_table, w, decay_logit, ret_gn):
    b, n, d = x.shape
    sh1, sc1, gt1, sh2, sc2, gt2 = [m.reshape(b, 1, d) for m in jnp.split(mod, N_MOD, axis=-1)]
    x2d = x.reshape(b * n, d)
    proj2d = _input_projection(x2d, n, sc1, sh1, w["g_pre_mix"], w["w_in"], cos, sin)
    proj = proj2d.reshape(b, n, IN_WIDTH)
    y_na = _neighbourhood_attention(proj, bias_table)
    y_ret, y_ret_b = _retention(proj, decay_logit)
    x1 = _merge(x2d, n, y_na.reshape(b * n, -1), y_ret.reshape(b * n, -1),
                y_ret_b.reshape(b * n, -1), proj2d, ret_gn,
                w["w_up_na"], w["w_up_ret"], w["w_out"], w["g_post_mix"], gt1)
    x2 = _ffn(x1, n, sc2, sh2, w["g_pre_ffn"], w["w_ffn_in"], w["w_ffn_out"], w["g_post_ffn"], gt2)
    return x2.reshape(b, n, d)


def kernel(x_prompt, x_sample, c_prompt, c_sample, w_mod, b_mod, g_pre_mix, w_in, rpb, ret_decay_logit,
           ret_gn, w_up_na, w_up_ret, w_out, g_post_mix, g_pre_ffn, w_ffn_in, w_ffn_out, g_post_ffn):
    depth = w_mod.shape[0]
    nb = x_prompt.shape[0]
    c_all = jnp.concatenate([c_prompt, c_sample], axis=0)
    c_all = jnp.pad(c_all, ((0, -c_all.shape[0] % 8), (0, 0)))
    cos, sin = _rope_tables(max(x_prompt.shape[1], x_sample.shape[1]))
    y_prompt, y_sample = x_prompt, x_sample
    for l in range(depth):
        q_cols = lax.broadcasted_iota(jnp.int32, (1, IN_WIDTH), 1) < NA_HEADS * NA_HEAD_DIM
        col_scale = jnp.where(q_cols, NA_HEAD_DIM ** -0.5 * LOG2_E, 1.0).astype(F32)
        w = {"g_pre_mix": g_pre_mix[l].reshape(1, -1), "w_in": (w_in[l] * col_scale).astype(BF16),
             "w_up_na": w_up_na[l].astype(BF16), "w_up_ret": w_up_ret[l].astype(BF16),
             "w_out": w_out[l].astype(BF16), "g_post_mix": g_post_mix[l].reshape(1, -1),
             "g_pre_ffn": g_pre_ffn[l].reshape(1, -1), "w_ffn_in": w_ffn_in[l].astype(BF16),
             "w_ffn_out": w_ffn_out[l].astype(BF16), "g_post_ffn": g_post_ffn[l].reshape(1, -1)}
        mod = _modulation(c_all, w_mod[l], b_mod[l])
        bias_table = _na_bias_table(rpb[l])
        y_prompt = _encoder_layer(y_prompt, mod[:nb], cos, sin, bias_table, w, ret_decay_logit[l], ret_gn[l])
        y_sample = _encoder_layer(y_sample, mod[nb:nb + x_sample.shape[0]], cos, sin, bias_table, w, ret_decay_logit[l], ret_gn[l])
    return (y_prompt, y_sample)
```
